```python
import math
import jax
import jax.numpy as jnp
from jax import lax
import numpy as np

D_MODEL = 1024
BATCH = 16
SEQ = 2048
DEPTH = 2
DEC_BATCH = 16
DEC_SEQ = 64
PAST_LEN = 2048

CHUNK = 64
EPS = 1e-6
NEG_INF = -1e30
A_HEADS = 4
A_DK = 128
A_DV = 128
A_QK = A_HEADS * A_DK
A_WIDTH = A_HEADS * A_DV
A_CONV = 4
B_HEADS = 4
B_DQK = 64
B_DV = 2 * B_DQK
B_QK = B_HEADS * 2 * B_DQK
B_WIDTH = B_HEADS * B_DV
Q_BLOCK = 128
C_WINDOWS = (2, 4, 8, 16)
C_GROUPS = 4
C_GROUP_DIM = 128
C_WIDTH = C_GROUPS * C_GROUP_DIM
C_POOL_BUF = 15
MIX_WIDTH = A_WIDTH + B_WIDTH + C_WIDTH
IN_WIDTHS = (A_QK, A_QK, A_WIDTH, A_WIDTH, A_HEADS, A_HEADS, B_QK, B_QK, B_WIDTH, C_WIDTH)
IN_WIDTH = 2 * A_QK + 2 * A_WIDTH + 2 * A_HEADS + 2 * B_QK + B_WIDTH + C_WIDTH
D_FF = 2816
FFN_CONV = 3

kernel_name = 'hybrid_stream_deltanet_diffattn_pool_step'


def rmsnorm(x, g):
    xf = x.astype(jnp.float32)
    y = xf * lax.rsqrt(jnp.mean(xf * xf, axis=-1, keepdims=True) + EPS)
    return (y * g.astype(jnp.float32)).astype(x.dtype)


def l2norm(x):
    xf = x.astype(jnp.float32)
    return xf * lax.rsqrt(jnp.sum(xf * xf, axis=-1, keepdims=True) + EPS)


def split_cols(t, widths):
    out, start = [], 0
    for w in widths:
        out.append(t[..., start:start + w])
        start += w
    return out


def causal_dwconv(x, buf, w, b):
    width = w.shape[0]
    length = x.shape[1]
    xp = jnp.concatenate([buf.astype(x.dtype), x], axis=1)
    y = b
    for i in range(width):
        y = y + xp[:, i:i + length] * w[i]
    return y, xp[:, xp.shape[1] - (width - 1):]


def gated_delta_rule(q, k, v, g, beta, s0):
    bsz, length, heads, dk = q.shape
    dv = v.shape[-1]
    n = -(-length // CHUNK)
    pad = n * CHUNK - length

    def prep(t):
        t = jnp.pad(t, [(0, 0), (0, pad)] + [(0, 0)] * (t.ndim - 2))
        t = t.reshape((bsz, n, CHUNK) + t.shape[2:])
        return jnp.moveaxis(t, 3, 1)

    q, k, v, g, beta = prep(q), prep(k), prep(v), prep(g), prep(beta)
    G = jnp.cumsum(g, axis=-1)
    idx = jnp.arange(CHUNK)
    incl = idx[:, None] >= idx[None, :]
    strict = idx[:, None] > idx[None, :]
    diff = G[..., :, None] - G[..., None, :]
    decay = jnp.where(incl, jnp.exp(jnp.where(incl, diff, 0.0)), 0.0)
    kk = jnp.einsum('bhncd,bhnsd->bhncs', k, k)
    lmat = jnp.where(strict, beta[..., :, None] * decay * kk, 0.0)
    eye = jnp.eye(CHUNK, dtype=jnp.float32)
    gam = jnp.exp(G)
    rhs = jnp.concatenate([v * beta[..., None], k * (beta * gam)[..., None]], axis=-1)
    uw = lax.linalg.triangular_solve(eye + lmat, rhs, left_side=True, lower=True, unit_diagonal=True)
    u_blk, w_blk = uw[..., :dv], uw[..., dv:]
    qk = jnp.einsum('bhncd,bhnsd->bhncs', q, k) * decay
    q_g = q * gam[..., None]
    k_tail = k * jnp.exp(G[..., -1:] - G)[..., None]
    g_last = gam[..., -1]

    def step(state, xs):
        u_c, w_c, qg_c, qk_c, kt_c, gl_c = xs
        delta = u_c - jnp.einsum('bhcd,bhde->bhce', w_c, state)
        o = jnp.einsum('bhcd,bhde->bhce', qg_c, state) + jnp.einsum('bhcs,bhse->bhce', qk_c, delta)
        state = state * gl_c[..., None, None] + jnp.einsum('bhcd,bhce->bhde', kt_c, delta)
        return state, o

    xs = tuple(jnp.moveaxis(t, 2, 0) for t in (u_blk, w_blk, q_g, qk, k_tail, g_last))
    s_final, o = lax.scan(step, s0, xs)
    o = jnp.moveaxis(jnp.moveaxis(o, 0, 2), 1, 3).reshape(bsz, n * CHUNK, heads, dv)
    return o[:, :length], s_final


def diff_softmax_weights(s, lam):
    p = jax.nn.softmax(s, axis=-1)
    return p[:, :, 0] - lam * p[:, :, 1]


def diff_attn_prompt(q, k, v, lam):
    bsz, length = q.shape[:2]
    nb = length // Q_BLOCK
    scale = B_DQK ** -0.5
    q_blocks = jnp.moveaxis(q.reshape(bsz, nb, Q_BLOCK, B_HEADS, 2, B_DQK), 1, 0)
    k_chunk = jnp.arange(length) // CHUNK

    def one_block(args):
        qi, bi = args
        q_chunk = (bi * Q_BLOCK + jnp.arange(Q_BLOCK)) // CHUNK
        mask = k_chunk[None, :] <= q_chunk[:, None]
        s = jnp.einsum('bqhmd,bkhmd->bhmqk', qi, k, preferred_element_type=jnp.float32) * scale
        s = jnp.where(mask, s, NEG_INF)
        w = diff_softmax_weights(s, lam)
        return jnp.einsum('bhqk,bkhe->bqhe', w.astype(v.dtype), v)

    o = lax.map(one_block, (q_blocks, jnp.arange(nb)))
    return jnp.moveaxis(o, 0, 1).reshape(bsz, length, B_HEADS, B_DV)


def diff_attn_sample(q, k_all, v_all, lam):
    s = jnp.einsum('bqhmd,bkhmd->bhmqk', q, k_all, preferred_element_type=jnp.float32) * (B_DQK ** -0.5)
    w = diff_softmax_weights(s, lam)
    return jnp.einsum('bhqk,bkhe->bqhe', w.astype(v_all.dtype), v_all)


def pool_mixer(xc, buf, pos0, c_w, c_scale):
    bsz, length, _ = xc.shape
    xp_raw = jnp.concatenate([buf.astype(xc.dtype), xc], axis=1)
    xp = xp_raw.astype(jnp.float32)
    cs = jnp.concatenate([jnp.zeros((bsz, 1, C_WIDTH), jnp.float32), jnp.cumsum(xp, axis=1)], axis=1)
    pos = pos0 + jnp.arange(length)
    hi = cs[:, C_POOL_BUF + 1:C_POOL_BUF + 1 + length]
    outs = []
    for gi, win in enumerate(C_WINDOWS):
        sl = slice(gi * C_GROUP_DIM, (gi + 1) * C_GROUP_DIM)
        lo = cs[:, C_POOL_BUF + 1 - win:C_POOL_BUF + 1 - win + length, sl]
        cnt = jnp.minimum(pos + 1, win).astype(jnp.float32)[None, :, None]
        pooled = (hi[..., sl] - lo) / cnt - xp[:, C_POOL_BUF:, sl]
        outs.append(jnp.einsum('blc,cd->bld', pooled, c_w[gi].astype(jnp.float32)))
    y = jnp.concatenate(outs, axis=-1) * c_scale.astype(jnp.float32)
    return y.astype(xc.dtype), xp_raw[:, xp_raw.shape[1] - C_POOL_BUF:]


def conv_ffn(h, buf, w_up, w_conv, b_conv, w_down):
    u = jnp.einsum('bld,df->blf', h, w_up)
    u, new_buf = causal_dwconv(u, buf, w_conv, b_conv)
    gate, val = split_cols(u, (D_FF, D_FF))
    return jnp.einsum('blf,fd->bld', jax.nn.silu(gate) * val, w_down), new_buf


def trunk_layer(x, pos0, delta_s0, conv_buf, past_k, past_v, pool_buf, ffn_buf,
                norm_mix, w_in, a_conv_w, a_conv_b, a_log, a_dt_bias, a_norm,
                b_lambda, b_norm, c_w, c_scale, w_out,
                norm_ffn, ffn_up, ffn_conv_w, ffn_conv_b, ffn_down, lam_init):
    bsz, length, _ = x.shape
    h = rmsnorm(x, norm_mix)
    proj = jnp.einsum('bld,de->ble', h, w_in)
    aq, ak, av, az, aa, ab, bq, bk, bv, cx = split_cols(proj, IN_WIDTHS)

    qkv, new_conv = causal_dwconv(jnp.concatenate([aq, ak, av], axis=-1), conv_buf, a_conv_w, a_conv_b)
    qkv = jax.nn.silu(qkv)
    q, k, v = split_cols(qkv, (A_QK, A_QK, A_WIDTH))
    q = l2norm(q.reshape(bsz, length, A_HEADS, A_DK)) * (A_DK ** -0.5)
    k = l2norm(k.reshape(bsz, length, A_HEADS, A_DK))
    v = v.reshape(bsz, length, A_HEADS, A_DV).astype(jnp.float32)
    g = -jnp.exp(a_log.astype(jnp.float32)) * jax.nn.softplus(aa.astype(jnp.float32) + a_dt_bias.astype(jnp.float32))
    beta = jax.nn.sigmoid(ab.astype(jnp.float32))
    o_a, new_delta = gated_delta_rule(q, k, v, g, beta, delta_s0.astype(jnp.float32))
    gate = jax.nn.silu(az.reshape(bsz, length, A_HEADS, A_DV).astype(jnp.float32))
    o_a = (rmsnorm(o_a, a_norm) * gate).reshape(bsz, length, A_WIDTH).astype(x.dtype)

    bq5 = bq.reshape(bsz, length, B_HEADS, 2, B_DQK)
    bk5 = bk.reshape(bsz, length, B_HEADS, 2, B_DQK)
    bv4 = bv.reshape(bsz, length, B_HEADS, B_DV)
    lf = b_lambda.astype(jnp.float32)
    lam = jnp.exp(jnp.sum(lf[0] * lf[1])) - jnp.exp(jnp.sum(lf[2] * lf[3])) + lam_init
    if past_k is None:
        o_b = diff_attn_prompt(bq5, bk5, bv4, lam)
    else:
        pk = past_k.reshape(past_k.shape[0], past_k.shape[1], B_HEADS, 2, B_DQK).astype(bk5.dtype)
        k_all = jnp.concatenate([pk, bk5], axis=1)
        v_all = jnp.concatenate([past_v.astype(bv4.dtype), bv4], axis=1)
        o_b = diff_attn_sample(bq5, k_all, v_all, lam)
    o_b = (rmsnorm(o_b, b_norm) * (1.0 - lam_init)).reshape(bsz, length, B_WIDTH).astype(x.dtype)
    new_k = bk.reshape(bsz, length, B_HEADS, 2 * B_DQK)
    new_v = bv4

    o_c, new_pool = pool_mixer(cx, pool_buf, pos0, c_w, c_scale)

    mixed = jnp.concatenate([o_a, o_b, o_c], axis=-1)
    x = x + jnp.einsum('ble,ed->bld', mixed, w_out)
    f, new_ffn = conv_ffn(rmsnorm(x, norm_ffn), ffn_buf, ffn_up, ffn_conv_w, ffn_conv_b, ffn_down)
    x = x + f
    return x, (new_delta.astype(x.dtype), new_conv, new_k, new_v, new_pool, new_ffn)


def setup_inputs(seed: int = 0) -> dict:
    key = jax.random.key(seed)
    ks = jax.random.split(key, 32)
    f32 = jnp.float32

    def nrm(k, shape, s):
        return jax.random.normal(k, shape, f32) * s

    a_log = jnp.log(jax.random.uniform(ks[12], (DEPTH, A_HEADS), f32, 1.0, 16.0))
    dt = jax.random.uniform(ks[13], (DEPTH, A_HEADS), f32, 0.001, 0.1)
    return {
        'x_prompt': nrm(ks[0], (BATCH, SEQ, D_MODEL), 1.0),
        'x_sample': nrm(ks[1], (DEC_BATCH, DEC_SEQ, D_MODEL), 1.0),
        'state_delta': nrm(ks[2], (DEPTH, DEC_BATCH, A_HEADS, A_DK, A_DV), 0.5),
        'cache_qkv_conv': nrm(ks[3], (DEPTH, DEC_BATCH, A_CONV - 1, 3 * A_WIDTH), 1.0),
        'cache_k': nrm(ks[4], (DEPTH, DEC_BATCH, PAST_LEN, B_HEADS, 2 * B_DQK), 1.0),
        'cache_v': nrm(ks[5], (DEPTH, DEC_BATCH, PAST_LEN, B_HEADS, B_DV), 1.0),
        'cache_pool': nrm(ks[6], (DEPTH, DEC_BATCH, C_POOL_BUF, C_WIDTH), 1.0),
        'cache_ffn_conv': nrm(ks[7], (DEPTH, DEC_BATCH, FFN_CONV - 1, 2 * D_FF), 1.0),
        'norm_mix': 1.0 + nrm(ks[8], (DEPTH, D_MODEL), 0.05),
        'w_in': nrm(ks[9], (DEPTH, D_MODEL, IN_WIDTH), D_MODEL ** -0.5),
        'a_conv_w': nrm(ks[10], (DEPTH, A_CONV, 3 * A_WIDTH), A_CONV ** -0.5),
        'a_conv_b': nrm(ks[11], (DEPTH, 3 * A_WIDTH), 0.02),
        'a_log': a_log,
        'a_dt_bias': jnp.log(jnp.expm1(dt)),
        'a_norm': 1.0 + nrm(ks[14], (DEPTH, A_DV), 0.05),
        'b_lambda': nrm(ks[15], (DEPTH, 4, B_DQK), 0.1),
        'b_norm': 1.0 + nrm(ks[16], (DEPTH, B_DV), 0.05),
        'c_w': nrm(ks[17], (DEPTH, C_GROUPS, C_GROUP_DIM, C_GROUP_DIM), C_GROUP_DIM ** -0.5),
        'c_scale': 1.0 + nrm(ks[18], (DEPTH, C_WIDTH), 0.1),
        'w_out': nrm(ks[19], (DEPTH, MIX_WIDTH, D_MODEL), MIX_WIDTH ** -0.5),
        'norm_ffn': 1.0 + nrm(ks[20], (DEPTH, D_MODEL), 0.05),
        'ffn_up': nrm(ks[21], (DEPTH, D_MODEL, 2 * D_FF), D_MODEL ** -0.5),
        'ffn_conv_w': nrm(ks[22], (DEPTH, FFN_CONV, 2 * D_FF), FFN_CONV ** -0.5),
        'ffn_conv_b': nrm(ks[23], (DEPTH, 2 * D_FF), 0.02),
        'ffn_down': nrm(ks[24], (DEPTH, D_FF, D_MODEL), D_FF ** -0.5),
        'norm_final': 1.0 + nrm(ks[25], (D_MODEL,), 0.05),
    }


def reference(x_prompt, x_sample, state_delta, cache_qkv_conv, cache_k, cache_v, cache_pool, cache_ffn_conv,
              norm_mix, w_in, a_conv_w, a_conv_b, a_log, a_dt_bias, a_norm, b_lambda, b_norm,
              c_w, c_scale, w_out, norm_ffn, ffn_up, ffn_conv_w, ffn_conv_b, ffn_down, norm_final):
    yp, ys = x_prompt, x_sample
    nbp = x_prompt.shape[0]
    p_new, s_new = [], []
    for l in range(DEPTH):
        lam_init = 0.8 - 0.6 * math.exp(-0.3 * l)
        w = (norm_mix[l], w_in[l], a_conv_w[l], a_conv_b[l], a_log[l], a_dt_bias[l], a_norm[l],
             b_lambda[l], b_norm[l], c_w[l], c_scale[l], w_out[l],
             norm_ffn[l], ffn_up[l], ffn_conv_w[l], ffn_conv_b[l], ffn_down[l])
        yp, sp = trunk_layer(
            yp, 0,
            jnp.zeros((nbp, A_HEADS, A_DK, A_DV), jnp.float32),
            jnp.zeros((nbp, A_CONV - 1, 3 * A_WIDTH), x_prompt.dtype),
            None, None,
            jnp.zeros((nbp, C_POOL_BUF, C_WIDTH), x_prompt.dtype),
            jnp.zeros((nbp, FFN_CONV - 1, 2 * D_FF), x_prompt.dtype),
            *w, lam_init)
        ys, ss = trunk_layer(
            ys, PAST_LEN, state_delta[l], cache_qkv_conv[l], cache_k[l], cache_v[l],
            cache_pool[l], cache_ffn_conv[l], *w, lam_init)
        p_new.append(sp)
        s_new.append(ss)
    y_prompt = rmsnorm(yp, norm_final)
    y_sample = rmsnorm(ys, norm_final)
    p_delta, p_conv, p_k, p_v, p_pool, p_ffn = [jnp.stack(t) for t in zip(*p_new)]
    s_delta, s_conv, s_k, s_v, s_pool, s_ffn = [jnp.stack(t) for t in zip(*s_new)]
    return (y_prompt, y_sample, p_delta, s_delta, p_conv, s_conv, p_k, s_k, p_v, s_v, p_pool, s_pool, p_ffn, s_ffn)
```

```python
import functools
import math

import jax
import jax.numpy as jnp
from jax import lax
from jax.experimental import pallas as pl
from jax.experimental.pallas import tpu as pltpu

F32 = jnp.float32
BF16 = jnp.bfloat16

CHUNK = 64
EPS = 1e-6
NEG_INF = -1e30
A_HEADS = 4
A_DK = 128
A_DV = 128
A_WIDTH = A_HEADS * A_DV
A_CONV = 4
B_HEADS = 4
B_DQK = 64
B_DV = 128
B_WIDTH = B_HEADS * B_DV
C_WINDOWS = (2, 4, 8, 16)
C_GROUP_DIM = 128
C_WIDTH = 512
C_POOL_BUF = 15
FFN_CONV = 3
LANES = 128
SUBLANES = 8
HALO = 16
VMEM_LIMIT = 56 * 1024 * 1024

PROJ_WIDTHS = (3 * A_WIDTH, A_WIDTH, LANES, B_WIDTH, B_WIDTH, B_WIDTH, C_WIDTH)


def _dot(a, b):
    return jnp.dot(a, b, preferred_element_type=F32)


def _dot_exact(a, b):
    return jnp.dot(a, b, preferred_element_type=F32, precision=lax.Precision.HIGHEST)


def _dot_nt(a, b):
    return lax.dot_general(a, b, (((1,), (1,)), ((), ())), preferred_element_type=F32)


def _dot_tn(a, b):
    return lax.dot_general(a, b, (((0,), (0,)), ((), ())), preferred_element_type=F32)


def _sigmoid(x):
    return 1.0 / (1.0 + jnp.exp(-x))


def _silu(x):
    return x * _sigmoid(x)


def _softplus(x):
    return jnp.maximum(x, 0.0) + jnp.log1p(jnp.exp(-jnp.abs(x)))


def _const_spec(shape):
    nd = len(shape)
    return pl.BlockSpec(shape, lambda *_: (0,) * nd, pipeline_mode=pl.Buffered(1))


def _params(*sem):
    return pltpu.CompilerParams(dimension_semantics=sem, vmem_limit_bytes=VMEM_LIMIT)


def _inproj_kernel(x_ref, g_ref, w_ref, *out_refs):
    x = x_ref[...]
    h = x * lax.rsqrt(jnp.mean(x * x, axis=-1, keepdims=True) + EPS) * g_ref[...]
    hb = h.astype(BF16)
    off = 0
    for ref in out_refs:
        n = ref.shape[-1]
        ref[...] = _dot(hb, w_ref[:, off:off + n]).astype(ref.dtype)
        off += n


def _inproj(x2d, g, w, tm):
    t, d = x2d.shape
    n_all = w.shape[1]
    return pl.pallas_call(
        _inproj_kernel,
        grid=(t // tm,),
        in_specs=[pl.BlockSpec((tm, d), lambda i: (i, 0)),
                  _const_spec((1, d)),
                  _const_spec((d, n_all))],
        out_specs=[pl.BlockSpec((tm, n), lambda i: (i, 0)) for n in PROJ_WIDTHS],
        out_shape=[jax.ShapeDtypeStruct((t, n), F32) for n in PROJ_WIDTHS],
        compiler_params=_params("parallel"),
        name="inproj",
    )(x2d, g, w)


def _deltanet_kernel(qkv_ref, z_ref, ab_ref, cw_ref, cb_ref, alog_ref, dtb_ref, anorm_ref,
                     s0_ref, c0_ref, o_ref, sout_ref, state_scr, conv_scr, *, n_chunks):
    j = pl.program_id(1)

    @pl.when(j == 0)
    def _():
        state_scr[...] = s0_ref[0]
        conv_scr[...] = c0_ref[0]

    row = lax.broadcasted_iota(jnp.int32, (CHUNK, CHUNK), 0)
    col = lax.broadcasted_iota(jnp.int32, (CHUNK, CHUNK), 1)
    incl = row >= col
    strict = row > col
    eye = (row == col).astype(F32)
    row_l = lax.broadcasted_iota(jnp.int32, (CHUNK, LANES), 0)
    neg_a = -jnp.exp(alog_ref[...])
    dtb = dtb_ref[...]
    anorm = anorm_ref[...]
    cbias = cb_ref[...]
    pad = SUBLANES - (A_CONV - 1)

    def chunk_body(c, carry):
        r0 = pl.multiple_of(c * CHUNK, CHUNK)
        x = qkv_ref[0, pl.ds(r0, CHUNK), :]
        xp = jnp.concatenate([conv_scr[...], x], axis=0)
        y = cbias
        for i in range(A_CONV):
            y = y + xp[pad + i:pad + i + CHUNK, :] * cw_ref[i:i + 1, :]
        conv_scr[...] = x[CHUNK - SUBLANES:, :]
        y = _silu(y)

        abv = ab_ref[0, pl.ds(r0, CHUNK), :]
        g = neg_a * _softplus(abv + dtb)
        beta = _sigmoid(abv)
        gsum = g
        s = 1
        while s < CHUNK:
            gsum = gsum + jnp.where(row_l >= s, pltpu.roll(gsum, s, 0), 0.0)
            s *= 2
        gsum_t = gsum.T
        zv = z_ref[0, pl.ds(r0, CHUNK), :]

        for h in range(A_HEADS):
            q = y[:, h * A_DK:(h + 1) * A_DK]
            k = y[:, A_WIDTH + h * A_DK:A_WIDTH + (h + 1) * A_DK]
            v = y[:, 2 * A_WIDTH + h * A_DV:2 * A_WIDTH + (h + 1) * A_DV]
            q = q * lax.rsqrt(jnp.sum(q * q, axis=-1, keepdims=True) + EPS) * (A_DK ** -0.5)
            k = k * lax.rsqrt(jnp.sum(k * k, axis=-1, keepdims=True) + EPS)
            gc = gsum[:, h:h + 1]
            gr = gsum_t[h:h + 1, :]
            bc = beta[:, A_HEADS + h:A_HEADS + h + 1]
            g_last = gsum[CHUNK - 1:CHUNK, h:h + 1]
            gam = jnp.exp(gc)
            decay = jnp.where(incl, jnp.exp(jnp.where(incl, gc - gr, 0.0)), 0.0)
            qb = q.astype(BF16)
            kb = k.astype(BF16)
            kk = _dot_nt(kb, kb)
            lmat = jnp.where(strict, bc * decay * kk, 0.0)
            npow = -lmat
            tinv = eye + npow
            p = 2
            while p < CHUNK:
                npow = _dot_exact(npow, npow)
                tinv = tinv + _dot_exact(tinv, npow)
                p *= 2
            rhs = jnp.concatenate([v * bc, k * (bc * gam)], axis=-1)
            uw = _dot_exact(tinv, rhs)
            u = uw[:, :A_DV]
            w = uw[:, A_DV:]
            qk = _dot_nt(qb, kb) * decay
            qg = q * gam
            k_tail = k * jnp.exp(g_last - gc)
            st = state_scr[h]
            stb = st.astype(BF16)
            delta = u - _dot(w.astype(BF16), stb)
            db = delta.astype(BF16)
            o = _dot(qg.astype(BF16), stb) + _dot(qk.astype(BF16), db)
            state_scr[h] = st * jnp.exp(g_last) + _dot_tn(k_tail.astype(BF16), db)
            on = o * lax.rsqrt(jnp.mean(o * o, axis=-1, keepdims=True) + EPS) * anorm
            gate = _silu(zv[:, h * A_DV:(h + 1) * A_DV])
            o_ref[0, pl.ds(r0, CHUNK), h * A_DV:(h + 1) * A_DV] = (on * gate).astype(o_ref.dtype)
        return carry

    lax.fori_loop(0, n_chunks, chunk_body, 0)

    @pl.when(j == pl.num_programs(1) - 1)
    def _():
        sout_ref[0] = state_scr[...]


def _deltanet(qkv, z, ab, cw, cb, alog, dtb, anorm, s0, c0, lb):
    b, l, _ = qkv.shape
    n_chunks = lb // CHUNK
    kern = functools.partial(_deltanet_kernel, n_chunks=n_chunks)
    return pl.pallas_call(
        kern,
        grid=(b, l // lb),
        in_specs=[pl.BlockSpec((1, lb, 3 * A_WIDTH), lambda i, j: (i, j, 0)),
                  pl.BlockSpec((1, lb, A_WIDTH), lambda i, j: (i, j, 0)),
                  pl.BlockSpec((1, lb, LANES), lambda i, j: (i, j, 0)),
                  _const_spec(cw.shape), _const_spec(cb.shape), _const_spec(alog.shape),
                  _const_spec(dtb.shape), _const_spec(anorm.shape),
                  pl.BlockSpec((1, A_HEADS, A_DK, A_DV), lambda i, j: (i, 0, 0, 0)),
                  pl.BlockSpec((1, SUBLANES, 3 * A_WIDTH), lambda i, j: (i, 0, 0))],
        out_specs=[pl.BlockSpec((1, lb, A_WIDTH), lambda i, j: (i, j, 0)),
                   pl.BlockSpec((1, A_HEADS, A_DK, A_DV), lambda i, j: (i, 0, 0, 0))],
        out_shape=[jax.ShapeDtypeStruct((b, l, A_WIDTH), F32),
                   jax.ShapeDtypeStruct((b, A_HEADS, A_DK, A_DV), F32)],
        scratch_shapes=[pltpu.VMEM((A_HEADS, A_DK, A_DV), F32),
                        pltpu.VMEM((SUBLANES, 3 * A_WIDTH), F32)],
        compiler_params=_params("parallel", "arbitrary"),
        name="deltanet",
    )(qkv, z, ab, cw, cb, alog, dtb, anorm, s0, c0)


def _lambda(bl_ref, lam_init):
    bl = bl_ref[...]
    s01 = jnp.sum(bl[0:1] * bl[1:2], axis=-1, keepdims=True)
    s23 = jnp.sum(bl[2:3] * bl[3:4], axis=-1, keepdims=True)
    return jnp.exp(s01) - jnp.exp(s23) + lam_init


def _split_maps(q):
    lane = lax.broadcasted_iota(jnp.int32, q.shape, 1)
    qs = q * (B_DQK ** -0.5)
    q1 = jnp.where(lane < B_DQK, qs, 0.0).astype(BF16)
    q2 = jnp.where(lane >= B_DQK, qs, 0.0).astype(BF16)
    return q1, q2


def _attn_finish(a1, l1, a2, l2, lam, bn, lam_init):
    o = a1 / l1 - lam * (a2 / l2)
    on = o * lax.rsqrt(jnp.mean(o * o, axis=-1, keepdims=True) + EPS) * bn
    return on * (1.0 - lam_init)


def _attn_prompt_kernel(q_ref, k_ref, v_ref, bl_ref, bn_ref, o_ref, kb_scr, vb_scr, *, lam_init, tq):
    qi = pl.program_id(2)

    @pl.when(qi == 0)
    def _():
        kb_scr[...] = k_ref[0].astype(BF16)
        vb_scr[...] = v_ref[0].astype(BF16)

    q1, q2 = _split_maps(q_ref[0])
    row = lax.broadcasted_iota(jnp.int32, (tq, tq), 0)
    col = lax.broadcasted_iota(jnp.int32, (tq, tq), 1)
    shift = CHUNK.bit_length() - 1
    visible = jnp.right_shift(col, shift) <= jnp.right_shift(row, shift)

    def update(s, vblk, m, l, a):
        mn = jnp.maximum(m, jnp.max(s, axis=-1, keepdims=True))
        alpha = jnp.exp(m - mn)
        p = jnp.exp(s - mn)
        l = alpha * l + jnp.sum(p, axis=-1, keepdims=True)
        a = alpha * a + _dot(p.astype(BF16), vblk)
        return mn, l, a

    def step(kb, carry, masked):
        m1, l1, a1, m2, l2, a2 = carry
        k0 = pl.multiple_of(kb * tq, tq)
        kblk = kb_scr[pl.ds(k0, tq), :]
        vblk = vb_scr[pl.ds(k0, tq), :]
        s1 = _dot_nt(q1, kblk)
        s2 = _dot_nt(q2, kblk)
        if masked:
            s1 = jnp.where(visible, s1, NEG_INF)
            s2 = jnp.where(visible, s2, NEG_INF)
        m1, l1, a1 = update(s1, vblk, m1, l1, a1)
        m2, l2, a2 = update(s2, vblk, m2, l2, a2)
        return m1, l1, a1, m2, l2, a2

    m0 = jnp.full((tq, 1), NEG_INF, F32)
    l0 = jnp.zeros((tq, 1), F32)
    a0 = jnp.zeros((tq, B_DV), F32)
    carry = lax.fori_loop(0, qi, lambda kb, c: step(kb, c, False), (m0, l0, a0, m0, l0, a0))
    m1, l1, a1, m2, l2, a2 = step(qi, carry, True)
    o_ref[0] = _attn_finish(a1, l1, a2, l2, _lambda(bl_ref, lam_init), bn_ref[...], lam_init).astype(o_ref.dtype)


def _attn_prompt(bq, bk, bv, bl, bn, lam_init, tq):
    b, l, _ = bq.shape
    kern = functools.partial(_attn_prompt_kernel, lam_init=lam_init, tq=tq)
    return pl.pallas_call(
        kern,
        grid=(b, B_HEADS, l // tq),
        in_specs=[pl.BlockSpec((1, tq, B_DV), lambda i, h, q: (i, q, h)),
                  pl.BlockSpec((1, l, B_DV), lambda i, h, q: (i, 0, h)),
                  pl.BlockSpec((1, l, B_DV), lambda i, h, q: (i, 0, h)),
                  _const_spec(bl.shape), _const_spec(bn.shape)],
        out_specs=pl.BlockSpec((1, tq, B_DV), lambda i, h, q: (i, q, h)),
        out_shape=jax.ShapeDtypeStruct((b, l, B_WIDTH), F32),
        scratch_shapes=[pltpu.VMEM((l, B_DV), BF16), pltpu.VMEM((l, B_DV), BF16)],
        compiler_params=_params("parallel", "parallel", "arbitrary"),
        name="attn_prompt",
    )(bq, bk, bv, bl, bn)


def _attn_sample_kernel(q_ref, kp_ref, vp_ref, kn_ref, vn_ref, bl_ref, bn_ref, o_ref, *, lam_init):
    q1, q2 = _split_maps(q_ref[0])
    kp = kp_ref[0].astype(BF16)
    vp = vp_ref[0].astype(BF16)
    kn = kn_ref[0].astype(BF16)
    vn = vn_ref[0].astype(BF16)

    def one_map(qm):
        sp = _dot_nt(qm, kp)
        sn = _dot_nt(qm, kn)
        m = jnp.maximum(jnp.max(sp, axis=-1, keepdims=True), jnp.max(sn, axis=-1, keepdims=True))
        pp = jnp.exp(sp - m)
        pn = jnp.exp(sn - m)
        l = jnp.sum(pp, axis=-1, keepdims=True) + jnp.sum(pn, axis=-1, keepdims=True)
        a = _dot(pp.astype(BF16), vp) + _dot(pn.astype(BF16), vn)
        return a, l

    a1, l1 = one_map(q1)
    a2, l2 = one_map(q2)
    o_ref[0] = _attn_finish(a1, l1, a2, l2, _lambda(bl_ref, lam_init), bn_ref[...], lam_init).astype(o_ref.dtype)


def _attn_sample(bq, bk, bv, past_k, past_v, bl, bn, lam_init):
    b, l, _ = bq.shape
    lp = past_k.shape[1]
    kern = functools.partial(_attn_sample_kernel, lam_init=lam_init)
    return pl.pallas_call(
        kern,
        grid=(b, B_HEADS),
        in_specs=[pl.BlockSpec((1, l, B_DV), lambda i, h: (i, 0, h)),
                  pl.BlockSpec((1, lp, B_DV), lambda i, h: (i, 0, h)),
                  pl.BlockSpec((1, lp, B_DV), lambda i, h: (i, 0, h)),
                  pl.BlockSpec((1, l, B_DV), lambda i, h: (i, 0, h)),
                  pl.BlockSpec((1, l, B_DV), lambda i, h: (i, 0, h)),
                  _const_spec(bl.shape), _const_spec(bn.shape)],
        out_specs=pl.BlockSpec((1, l, B_DV), lambda i, h: (i, 0, h)),
        out_shape=jax.ShapeDtypeStruct((b, l, B_WIDTH), F32),
        compiler_params=_params("parallel", "parallel"),
        name="attn_sample",
    )(bq, past_k, past_v, bk, bv, bl, bn)


def _pool_kernel(x_ref, h0_ref, cw_ref, cs_ref, o_ref, halo_scr, *, pos0, lt):
    j = pl.program_id(1)

    @pl.when(j == 0)
    def _():
        halo_scr[...] = h0_ref[0]

    x = x_ref[0]
    xp = jnp.concatenate([halo_scr[...], x], axis=0)
    halo_scr[...] = x[lt - HALO:, :]
    pos = pos0 + j * lt + lax.broadcasted_iota(jnp.int32, (lt, 1), 0)
    for gi, win in enumerate(C_WINDOWS):
        sl = slice(gi * C_GROUP_DIM, (gi + 1) * C_GROUP_DIM)
        acc = xp[:, sl]
        s = 1
        while s < win:
            acc = acc + pltpu.roll(acc, s, 0)
            s *= 2
        cnt = jnp.minimum(pos + 1, win).astype(F32)
        pooled = acc[HALO:, :] / cnt - x[:, sl]
        y = _dot(pooled.astype(BF16), cw_ref[gi]) * cs_ref[:, sl]
        o_ref[0, :, sl] = y.astype(o_ref.dtype)


def _pool(cx, halo0, cw, cs, pos0, lt):
    b, l, _ = cx.shape
    kern = functools.partial(_pool_kernel, pos0=pos0, lt=lt)
    return pl.pallas_call(
        kern,
        grid=(b, l // lt),
        in_specs=[pl.BlockSpec((1, lt, C_WIDTH), lambda i, j: (i, j, 0)),
                  pl.BlockSpec((1, HALO, C_WIDTH), lambda i, j: (i, 0, 0)),
                  _const_spec(cw.shape), _const_spec(cs.shape)],
        out_specs=pl.BlockSpec((1, lt, C_WIDTH), lambda i, j: (i, j, 0)),
        out_shape=jax.ShapeDtypeStruct((b, l, C_WIDTH), F32),
        scratch_shapes=[pltpu.VMEM((HALO, C_WIDTH), F32)],
        compiler_params=_params("parallel", "arbitrary"),
        name="pool",
    )(cx, halo0, cw, cs)


def _outffn_kernel(x_ref, oa_ref, ob_ref, oc_ref, wo_ref, gf_ref, up_ref, fw_ref, fb_ref, dn_ref,
                   f0_ref, gfin_ref, y_ref, fnew_ref, halo_scr, *, seg_len, tiles_per_seq, final_norm):
    i = pl.program_id(0)
    tm = x_ref.shape[0]
    nseg = tm // seg_len
    d_ff = dn_ref.shape[0]
    hw = FFN_CONV - 1

    if tiles_per_seq > 1:
        @pl.when(i % tiles_per_seq == 0)
        def _():
            halo_scr[...] = f0_ref[0]

    x1 = x_ref[...]
    x1 = x1 + _dot(oa_ref[...].astype(BF16), wo_ref[0:A_WIDTH, :])
    x1 = x1 + _dot(ob_ref[...].astype(BF16), wo_ref[A_WIDTH:A_WIDTH + B_WIDTH, :])
    x1 = x1 + _dot(oc_ref[...].astype(BF16), wo_ref[A_WIDTH + B_WIDTH:, :])
    h = x1 * lax.rsqrt(jnp.mean(x1 * x1, axis=-1, keepdims=True) + EPS) * gf_ref[...]
    u = _dot(h.astype(BF16), up_ref[...])

    row = lax.broadcasted_iota(jnp.int32, (seg_len, 1), 0)
    acts = []
    for s in range(nseg):
        us = u[s * seg_len:(s + 1) * seg_len, :]
        prev = halo_scr[...] if tiles_per_seq > 1 else f0_ref[s]
        c = fb_ref[...]
        for t in range(FFN_CONV):
            sh = hw - t
            if sh == 0:
                tap = us
            else:
                rolled = pltpu.roll(us, sh, 0)
                for r in range(sh):
                    rolled = jnp.where(row == r, prev[hw - sh + r:hw - sh + r + 1, :], rolled)
                tap = rolled
            c = c + tap * fw_ref[t:t + 1, :]
        acts.append((_silu(c[:, :d_ff]) * c[:, d_ff:]).astype(BF16))
        fnew_ref[s] = us[seg_len - hw:, :]
    if tiles_per_seq > 1:
        halo_scr[...] = u[tm - hw:, :]
    act = acts[0] if nseg == 1 else jnp.concatenate(acts, axis=0)
    x2 = x1 + _dot(act, dn_ref[...])
    if final_norm:
        x2 = x2 * lax.rsqrt(jnp.mean(x2 * x2, axis=-1, keepdims=True) + EPS) * gfin_ref[...]
    y_ref[...] = x2.astype(y_ref.dtype)


def _outffn(x2d, oa, ob, oc, wo, gf, up, fw, fb, dn, f0, gfin, seq_len, tm, final_norm):
    t, d = x2d.shape
    nb = f0.shape[0]
    two_ff = up.shape[1]
    seg_len = min(tm, seq_len)
    tiles_per_seq = max(1, seq_len // tm)
    nseg = tm // seg_len
    hw = FFN_CONV - 1
    kern = functools.partial(_outffn_kernel, seg_len=seg_len, tiles_per_seq=tiles_per_seq,
                             final_norm=final_norm)
    tok = lambda n: pl.BlockSpec((tm, n), lambda i: (i, 0))
    seq_map = lambda i: (i // tiles_per_seq, 0, 0)
    return pl.pallas_call(
        kern,
        grid=(t // tm,),
        in_specs=[tok(d), tok(A_WIDTH), tok(B_WIDTH), tok(C_WIDTH),
                  _const_spec(wo.shape), _const_spec(gf.shape), _const_spec(up.shape),
                  _const_spec(fw.shape), _const_spec(fb.shape), _const_spec(dn.shape),
                  pl.BlockSpec((nseg, hw, two_ff), seq_map),
                  _const_spec(gfin.shape)],
        out_specs=[tok(d), pl.BlockSpec((nseg, hw, two_ff), seq_map)],
        out_shape=[jax.ShapeDtypeStruct((t, d), F32),
                   jax.ShapeDtypeStruct((nb, hw, two_ff), F32)],
        scratch_shapes=[pltpu.VMEM((hw, two_ff), F32)],
        compiler_params=_params("arbitrary"),
        name="outffn",
    )(x2d, oa, ob, oc, wo, gf, up, fw, fb, dn, f0, gfin)


def _pad_rows_top(a, rows):
    return jnp.pad(a, ((0, 0), (rows - a.shape[1], 0), (0, 0)))


def _layer(x, pos0, s0, conv0, past_k, past_v, pool0, ffn0, wts, lam_init, final_norm, gfin):
    (g_mix, w_in, cw, cb, alog, dtb, anorm, bl, bn, c_w, c_s, wo, g_ffn, up, fw, fb, dn) = wts
    b, l, d = x.shape
    t = b * l
    x2d = x.reshape(t, d)
    tm = min(256, t)
    qkv, z, ab, bq, bk, bv, cx = _inproj(x2d, g_mix, w_in, tm)
    r3 = lambda a: a.reshape(b, l, a.shape[-1])
    qkv, z, ab, bq, bk, bv, cx = map(r3, (qkv, z, ab, bq, bk, bv, cx))

    o_a, new_state = _deltanet(qkv, z, ab, cw, cb, alog, dtb, anorm, s0,
                               _pad_rows_top(conv0, SUBLANES), min(l, 512))
    if past_k is None:
        o_b = _attn_prompt(bq, bk, bv, bl, bn, lam_init, min(l, 256))
    else:
        o_b = _attn_sample(bq, bk, bv, past_k.reshape(b, -1, B_WIDTH), past_v.reshape(b, -1, B_WIDTH),
                           bl, bn, lam_init)
    o_c = _pool(cx, _pad_rows_top(pool0, HALO), c_w, c_s, pos0, min(l, 512))

    y, new_ffn = _outffn(x2d, o_a.reshape(t, -1), o_b.reshape(t, -1), o_c.reshape(t, -1),
                         wo, g_ffn, up, fw, fb, dn, ffn0, gfin, l, tm, final_norm)
    new = (new_state, qkv[:, l - (A_CONV - 1):, :], bk.reshape(b, l, B_HEADS, 2 * B_DQK),
           bv.reshape(b, l, B_HEADS, B_DV), cx[:, l - C_POOL_BUF:, :], new_ffn)
    return y.reshape(b, l, d), new


def _prep_weights(l, norm_mix, w_in, a_conv_w, a_conv_b, a_log, a_dt_bias, a_norm, b_lambda, b_norm,
                  c_w, c_scale, w_out, norm_ffn, ffn_up, ffn_conv_w, ffn_conv_b, ffn_down):
    w = w_in[l]
    n_qkvz = 4 * A_WIDTH
    w_ab = jnp.pad(w[:, n_qkvz:n_qkvz + 2 * A_HEADS], ((0, 0), (0, LANES - 2 * A_HEADS)))
    w_packed = jnp.concatenate([w[:, :n_qkvz], w_ab, w[:, n_qkvz + 2 * A_HEADS:]], axis=1).astype(BF16)
    lane_pad = lambda v: jnp.pad(v.reshape(1, -1), ((0, 0), (0, LANES - v.shape[-1])))
    return (norm_mix[l].reshape(1, -1), w_packed, a_conv_w[l], a_conv_b[l].reshape(1, -1),
            lane_pad(a_log[l]), lane_pad(a_dt_bias[l]), a_norm[l].reshape(1, -1),
            b_lambda[l], b_norm[l].reshape(1, -1), c_w[l].astype(BF16), c_scale[l].reshape(1, -1),
            w_out[l].astype(BF16), norm_ffn[l].reshape(1, -1), ffn_up[l].astype(BF16),
            ffn_conv_w[l], ffn_conv_b[l].reshape(1, -1), ffn_down[l].astype(BF16))


def kernel(x_prompt, x_sample, state_delta, cache_qkv_conv, cache_k, cache_v, cache_pool, cache_ffn_conv,
           norm_mix, w_in, a_conv_w, a_conv_b, a_log, a_dt_bias, a_norm, b_lambda, b_norm, c_w, c_scale,
           w_out, norm_ffn, ffn_up, ffn_conv_w, ffn_conv_b, ffn_down, norm_final):
    depth = w_in.shape[0]
    nbp = x_prompt.shape[0]
    two_ff = ffn_up.shape[-1]
    gfin = norm_final.reshape(1, -1)
    yp, ys = x_prompt, x_sample
    p_new, s_new = [], []
    for l in range(depth):
        lam_init = 0.8 - 0.6 * math.exp(-0.3 * l)
        wts = _prep_weights(l, norm_mix, w_in, a_conv_w, a_conv_b, a_log, a_dt_bias, a_norm, b_lambda,
                            b_norm, c_w, c_scale, w_out, norm_ffn, ffn_up, ffn_conv_w, ffn_conv_b, ffn_down)
        final = l == depth - 1
        yp, sp = _layer(
            yp, 0,
            jnp.zeros((nbp, A_HEADS, A_DK, A_DV), F32),
            jnp.zeros((nbp, A_CONV - 1, 3 * A_WIDTH), F32),
            None, None,
            jnp.zeros((nbp, C_POOL_BUF, C_WIDTH), F32),
            jnp.zeros((nbp, FFN_CONV - 1, two_ff), F32),
            wts, lam_init, final, gfin)
        ys, ss = _layer(
            ys, cache_k.shape[2], state_delta[l], cache_qkv_conv[l], cache_k[l], cache_v[l],
            cache_pool[l], cache_ffn_conv[l], wts, lam_init, final, gfin)
        p_new.append(sp)
        s_new.append(ss)
    p_delta, p_conv, p_k, p_v, p_pool, p_ffn = [jnp.stack(t) for t in zip(*p_new)]
    s_delta, s_conv, s_k, s_v, s_pool, s_ffn = [jnp.stack(t) for t in zip(*s_new)]
    return (yp, ys, p_delta, s_delta, p_conv, s_conv, p_k, s_k, p_v, s_v, p_pool, s_pool, p_ffn, s_ffn)
```

```python
import functools
import math

import jax
import jax.numpy as jnp
from jax import lax
from jax.experimental import pallas as pl
from jax.experimental.pallas import tpu as pltpu

F32 = jnp.float32
BF16 = jnp.bfloat16

CHUNK = 64
EPS = 1e-6
NEG_INF = -1e30
A_HEADS = 4
A_DK = 128
A_DV = 128
A_WIDTH = A_HEADS * A_DV
A_CONV = 4
B_HEADS = 4
B_DQK = 64
B_DV = 128
B_WIDTH = B_HEADS * B_DV
C_WINDOWS = (2, 4, 8, 16)
C_GROUP_DIM = 128
C_WIDTH = 512
C_POOL_BUF = 15
FFN_CONV = 3
LANES = 128
SUBLANES = 8
HALO = 16
VMEM_LIMIT = 56 * 1024 * 1024

PROJ_WIDTHS = (3 * A_WIDTH, A_WIDTH, LANES, B_WIDTH, B_WIDTH, B_WIDTH, C_WIDTH)


def _dot(a, b):
    return jnp.dot(a, b, preferred_element_type=F32)


def _dot_exact(a, b):
    return jnp.dot(a, b, preferred_element_type=F32, precision=lax.Precision.HIGHEST)


def _dot_nt(a, b):
    return lax.dot_general(a, b, (((1,), (1,)), ((), ())), preferred_element_type=F32)


def _dot_tn(a, b):
    return lax.dot_general(a, b, (((0,), (0,)), ((), ())), preferred_element_type=F32)


def _sigmoid(x):
    return 1.0 / (1.0 + jnp.exp(-x))


def _silu(x):
    return x * _sigmoid(x)


def _softplus(x):
    return jnp.maximum(x, 0.0) + jnp.log1p(jnp.exp(-jnp.abs(x)))


def _const_spec(shape):
    nd = len(shape)
    return pl.BlockSpec(shape, lambda *_: (0,) * nd, pipeline_mode=pl.Buffered(1))


def _params(*sem):
    return pltpu.CompilerParams(dimension_semantics=sem, vmem_limit_bytes=VMEM_LIMIT)


def _inproj_kernel(x_ref, g_ref, w_ref, *out_refs):
    x = x_ref[...]
    h = x * lax.rsqrt(jnp.mean(x * x, axis=-1, keepdims=True) + EPS) * g_ref[...]
    hb = h.astype(BF16)
    off = 0
    for ref in out_refs:
        n = ref.shape[-1]
        ref[...] = _dot(hb, w_ref[:, off:off + n]).astype(ref.dtype)
        off += n


def _inproj(x2d, g, w, tm):
    t, d = x2d.shape
    n_all = w.shape[1]
    return pl.pallas_call(
        _inproj_kernel,
        grid=(t // tm,),
        in_specs=[pl.BlockSpec((tm, d), lambda i: (i, 0)),
                  _const_spec((1, d)),
                  _const_spec((d, n_all))],
        out_specs=[pl.BlockSpec((tm, n), lambda i: (i, 0)) for n in PROJ_WIDTHS],
        out_shape=[jax.ShapeDtypeStruct((t, n), F32) for n in PROJ_WIDTHS],
        compiler_params=_params("parallel"),
        name="inproj",
    )(x2d, g, w)


def _split_bf16(x):
    hi = x.astype(BF16)
    lo = (x - hi.astype(F32)).astype(BF16)
    return hi, lo


def _deltanet_kernel(qkv_ref, z_ref, ab_ref, cw_ref, cb_ref, alog_ref, dtb_ref, anorm_ref,
                     s0_ref, c0_ref, o_ref, sout_ref,
                     state_scr, x_scr, u_scr, wq_scr, kt_scr, qk_scr, gl_scr, *, n_chunks):
    j = pl.program_id(1)
    lb = n_chunks * CHUNK
    wide = A_HEADS * CHUNK
    log_chunk = CHUNK.bit_length() - 1

    @pl.when(j == 0)
    def _():
        state_scr[...] = s0_ref[0]
        x_scr[0:SUBLANES, :] = c0_ref[0]

    @pl.when(j > 0)
    def _():
        x_scr[0:SUBLANES, :] = x_scr[lb:lb + SUBLANES, :]

    x_scr[SUBLANES:SUBLANES + lb, :] = qkv_ref[0]

    row_w = lax.broadcasted_iota(jnp.int32, (CHUNK, wide), 0)
    lane_w = lax.broadcasted_iota(jnp.int32, (CHUNK, wide), 1)
    col_w = jnp.bitwise_and(lane_w, CHUNK - 1)
    grp_w = jnp.right_shift(lane_w, log_chunk)
    incl_w = row_w >= col_w
    strict_w = row_w > col_w
    eye_w = row_w == col_w
    row_l = lax.broadcasted_iota(jnp.int32, (CHUNK, LANES), 0)
    bd_mask = (jnp.right_shift(lax.broadcasted_iota(jnp.int32, (wide, wide), 0), log_chunk)
               == jnp.right_shift(lax.broadcasted_iota(jnp.int32, (wide, wide), 1), log_chunk))
    kd_mask = (jnp.right_shift(lax.broadcasted_iota(jnp.int32, (wide, A_WIDTH), 0), log_chunk)
               == jnp.right_shift(lax.broadcasted_iota(jnp.int32, (wide, A_WIDTH), 1), A_DK.bit_length() - 1))
    neg_a = -jnp.exp(alog_ref[...])
    dtb = dtb_ref[...]
    anorm = anorm_ref[...]
    cbias = cb_ref[...]
    pad = SUBLANES - (A_CONV - 1)

    def block_diag(xw):
        return jnp.where(bd_mask, jnp.concatenate([xw] * A_HEADS, axis=0), jnp.zeros((), xw.dtype))

    def widen(x, base):
        out = jnp.broadcast_to(x[:, base:base + 1], (CHUNK, wide))
        for h in range(1, A_HEADS):
            out = jnp.where(grp_w == h, x[:, base + h:base + h + 1], out)
        return out

    def mm_wide(x, yw):
        m = x.shape[0]
        xh, xl = _split_bf16(x)
        yh, yl = _split_bf16(yw)
        top = _dot(jnp.concatenate([xh, xl], axis=0), block_diag(yh))
        return top[:m] + top[m:] + _dot(xh, block_diag(yl))

    group = 2 if n_chunks % 2 == 0 else 1

    def pass1_front(c):
        r0 = pl.multiple_of(c * CHUNK, CHUNK)
        rows = pl.ds(r0, CHUNK)
        xp = x_scr[pl.ds(r0, SUBLANES + CHUNK), :]
        y = cbias
        for i in range(A_CONV):
            y = y + xp[pad + i:pad + i + CHUNK, :] * cw_ref[i:i + 1, :]
        y = _silu(y)
        qs, ks = [], []
        for h in range(A_HEADS):
            q = y[:, h * A_DK:(h + 1) * A_DK]
            k = y[:, A_WIDTH + h * A_DK:A_WIDTH + (h + 1) * A_DK]
            qs.append(q * lax.rsqrt(jnp.sum(q * q, axis=-1, keepdims=True) + EPS) * (A_DK ** -0.5))
            ks.append(k * lax.rsqrt(jnp.sum(k * k, axis=-1, keepdims=True) + EPS))
        qn = jnp.concatenate(qs, axis=-1)
        kn = jnp.concatenate(ks, axis=-1)

        abv = ab_ref[0, rows, :]
        g = neg_a * _softplus(abv + dtb)
        beta = _sigmoid(abv)
        gsum = g
        s = 1
        while s < CHUNK:
            gsum = gsum + jnp.where(row_l >= s, pltpu.roll(gsum, s, 0), 0.0)
            s *= 2
        gc_w = widen(gsum, 0)
        bc_w = widen(beta, A_HEADS)
        gr_w = jnp.sum(jnp.where(eye_w, gc_w, 0.0), axis=0, keepdims=True)
        decay_w = jnp.where(incl_w, jnp.exp(jnp.where(incl_w, gc_w - gr_w, 0.0)), 0.0)

        qb = qn.astype(BF16)
        kb = kn.astype(BF16)
        kdt = jnp.where(kd_mask, jnp.concatenate([kb] * A_HEADS, axis=0), jnp.zeros((), BF16))
        qkk = _dot_nt(jnp.concatenate([qb, kb], axis=0), kdt)
        qk_w = qkk[:CHUNK] * decay_w
        npow = jnp.where(strict_w, -(bc_w * decay_w * qkk[CHUNK:]), 0.0)
        return dict(c=c, rows=rows, y=y, qs=qs, ks=ks, gsum=gsum, beta=beta, qk_w=qk_w, npow=npow)

    def pass1_back(f, tinv):
        c, rows, y, qs, ks, gsum, beta, qk_w = (f[n] for n in ("c", "rows", "y", "qs", "ks", "gsum", "beta", "qk_w"))
        t_low = jnp.where(eye_w, 0.0, tinv)
        g_last = gsum[CHUNK - 1:CHUNK, :]
        gl_scr[c] = jnp.exp(g_last)
        for h in range(A_HEADS):
            hs = slice(h * A_DK, (h + 1) * A_DK)
            cs = slice(h * CHUNK, (h + 1) * CHUNK)
            v = y[:, 2 * A_WIDTH + h * A_DV:2 * A_WIDTH + (h + 1) * A_DV]
            gc = gsum[:, h:h + 1]
            bc = beta[:, A_HEADS + h:A_HEADS + h + 1]
            gam = jnp.exp(gc)
            rhs = jnp.concatenate([v * bc, ks[h] * (bc * gam)], axis=-1)
            uw = rhs + _dot(t_low[:, cs].astype(BF16), rhs.astype(BF16))
            qk_scr[c, h] = qk_w[:, cs].astype(BF16)
            u_scr[rows, hs] = uw[:, :A_DV]
            wq_scr[c, h] = jnp.concatenate([uw[:, A_DV:], qs[h] * gam], axis=0).astype(BF16)
            kt_scr[rows, hs] = (ks[h] * jnp.exp(g_last[:, h:h + 1] - gc)).astype(BF16)

    def pass1(i, carry):
        fronts = [pass1_front(i * group + t) for t in range(group)]
        npows = [f["npow"] for f in fronts]
        tinvs = [jnp.where(eye_w, 1.0, n) for n in npows]
        npows = [mm_wide(n, n) for n in npows]
        p = 2
        while 2 * p < CHUNK:
            boths = [mm_wide(jnp.concatenate([n, t], axis=0), n) for n, t in zip(npows, tinvs)]
            npows = [b[:CHUNK] for b in boths]
            tinvs = [t + b[CHUNK:] for t, b in zip(tinvs, boths)]
            p *= 2
        tinvs = [t + mm_wide(t, n) for n, t in zip(npows, tinvs)]
        for f, t in zip(fronts, tinvs):
            pass1_back(f, t)
        return carry

    def pass2(c, carry):
        r0 = pl.multiple_of(c * CHUNK, CHUNK)
        rows = pl.ds(r0, CHUNK)
        zv = z_ref[0, rows, :]
        glv = gl_scr[c]
        heads = range(A_HEADS)
        hsl = [slice(h * A_DK, (h + 1) * A_DK) for h in heads]
        sts = [state_scr[h] for h in heads]
        wss = [_dot(wq_scr[c, h], sts[h].astype(BF16)) for h in heads]
        dbs = [(u_scr[rows, hsl[h]] - wss[h][:CHUNK]).astype(BF16) for h in heads]
        outs = [wss[h][CHUNK:] + _dot(qk_scr[c, h], dbs[h]) for h in heads]
        for h in heads:
            state_scr[h] = sts[h] * glv[:, h:h + 1] + _dot_tn(kt_scr[rows, hsl[h]], dbs[h])
        for h in heads:
            o = outs[h]
            on = o * lax.rsqrt(jnp.mean(o * o, axis=-1, keepdims=True) + EPS) * anorm
            gate = _silu(zv[:, hsl[h]])
            o_ref[0, rows, hsl[h]] = (on * gate).astype(o_ref.dtype)
        return carry

    lax.fori_loop(0, n_chunks // group, pass1, 0)
    lax.fori_loop(0, n_chunks, pass2, 0)

    @pl.when(j == pl.num_programs(1) - 1)
    def _():
        sout_ref[0] = state_scr[...]


def _deltanet(qkv, z, ab, cw, cb, alog, dtb, anorm, s0, c0, lb):
    b, l, _ = qkv.shape
    n_chunks = lb // CHUNK
    kern = functools.partial(_deltanet_kernel, n_chunks=n_chunks)
    return pl.pallas_call(
        kern,
        grid=(b, l // lb),
        in_specs=[pl.BlockSpec((1, lb, 3 * A_WIDTH), lambda i, j: (i, j, 0)),
                  pl.BlockSpec((1, lb, A_WIDTH), lambda i, j: (i, j, 0)),
                  pl.BlockSpec((1, lb, LANES), lambda i, j: (i, j, 0)),
                  _const_spec(cw.shape), _const_spec(cb.shape), _const_spec(alog.shape),
                  _const_spec(dtb.shape), _const_spec(anorm.shape),
                  pl.BlockSpec((1, A_HEADS, A_DK, A_DV), lambda i, j: (i, 0, 0, 0)),
                  pl.BlockSpec((1, SUBLANES, 3 * A_WIDTH), lambda i, j: (i, 0, 0))],
        out_specs=[pl.BlockSpec((1, lb, A_WIDTH), lambda i, j: (i, j, 0)),
                   pl.BlockSpec((1, A_HEADS, A_DK, A_DV), lambda i, j: (i, 0, 0, 0))],
        out_shape=[jax.ShapeDtypeStruct((b, l, A_WIDTH), F32),
                   jax.ShapeDtypeStruct((b, A_HEADS, A_DK, A_DV), F32)],
        scratch_shapes=[pltpu.VMEM((A_HEADS, A_DK, A_DV), F32),
                        pltpu.VMEM((SUBLANES + lb, 3 * A_WIDTH), F32),
                        pltpu.VMEM((lb, A_WIDTH), F32),
                        pltpu.VMEM((n_chunks, A_HEADS, 2 * CHUNK, A_DK), BF16),
                        pltpu.VMEM((lb, A_WIDTH), BF16),
                        pltpu.VMEM((n_chunks, A_HEADS, CHUNK, CHUNK), BF16),
                        pltpu.VMEM((n_chunks, 1, LANES), F32)],
        compiler_params=_params("parallel", "arbitrary"),
        name="deltanet",
    )(qkv, z, ab, cw, cb, alog, dtb, anorm, s0, c0)


def _lambda(bl_ref, lam_init):
    bl = bl_ref[...]
    s01 = jnp.sum(bl[0:1] * bl[1:2], axis=-1, keepdims=True)
    s23 = jnp.sum(bl[2:3] * bl[3:4], axis=-1, keepdims=True)
    return jnp.exp(s01) - jnp.exp(s23) + lam_init


def _split_maps(q):
    lane = lax.broadcasted_iota(jnp.int32, q.shape, 1)
    qs = q * (B_DQK ** -0.5)
    q1 = jnp.where(lane < B_DQK, qs, 0.0).astype(BF16)
    q2 = jnp.where(lane >= B_DQK, qs, 0.0).astype(BF16)
    return q1, q2


def _attn_finish(a1, l1, a2, l2, lam, bn, lam_init):
    o = a1 / l1 - lam * (a2 / l2)
    on = o * lax.rsqrt(jnp.mean(o * o, axis=-1, keepdims=True) + EPS) * bn
    return on * (1.0 - lam_init)


ONES_ROWS = 16


def _attn_prompt_kernel(q_ref, k_ref, v_ref, bl_ref, bn_ref, o_ref, kb_scr, vt_scr, acc_scr, *, lam_init, tq):
    qi = pl.program_id(2)
    nblk = kb_scr.shape[0] // tq

    @pl.when(qi == 0)
    def _():
        kb_scr[...] = k_ref[0].astype(BF16)
        ones = jnp.ones((ONES_ROWS, tq), BF16)
        for i in range(nblk):
            vt = v_ref[0, i * tq:(i + 1) * tq, :].T.astype(BF16)
            vt_scr[i] = jnp.concatenate([vt, ones], axis=0)

    q1, q2 = _split_maps(q_ref[0])
    key = lax.broadcasted_iota(jnp.int32, (tq, tq), 0)
    qry = lax.broadcasted_iota(jnp.int32, (tq, tq), 1)
    shift = CHUNK.bit_length() - 1
    visible = jnp.right_shift(key, shift) <= jnp.right_shift(qry, shift)

    def step(kb, ms, masked):
        k0 = pl.multiple_of(kb * tq, tq)
        kblk = kb_scr[pl.ds(k0, tq), :]
        vt = vt_scr[kb]
        sts = [_dot_nt(kblk, q1), _dot_nt(kblk, q2)]
        if masked:
            sts = [jnp.where(visible, s, NEG_INF) for s in sts]
        mns = [jnp.maximum(m, jnp.max(s, axis=0, keepdims=True)) for m, s in zip(ms, sts)]
        pts = [jnp.exp(s - mn).astype(BF16) for s, mn in zip(sts, mns)]
        upd = [_dot(vt, p) for p in pts]
        for t in range(2):
            acc_scr[t] = jnp.exp(ms[t] - mns[t]) * acc_scr[t] + upd[t]
        return tuple(mns)

    acc_scr[...] = jnp.zeros_like(acc_scr)
    m0 = jnp.full((1, tq), NEG_INF, F32)
    ms = lax.fori_loop(0, qi, lambda kb, c: step(kb, c, False), (m0, m0))
    step(qi, ms, True)
    a1, a2 = acc_scr[0], acc_scr[1]
    ot = (a1[:B_DV] / a1[B_DV:B_DV + 1]
          - _lambda(bl_ref, lam_init) * (a2[:B_DV] / a2[B_DV:B_DV + 1]))
    ont = ot * lax.rsqrt(jnp.mean(ot * ot, axis=0, keepdims=True) + EPS)
    o_ref[0] = (ont.T * bn_ref[...] * (1.0 - lam_init)).astype(o_ref.dtype)


def _attn_prompt(bq, bk, bv, bl, bn, lam_init, tq):
    b, l, _ = bq.shape
    kern = functools.partial(_attn_prompt_kernel, lam_init=lam_init, tq=tq)
    return pl.pallas_call(
        kern,
        grid=(b, B_HEADS, l // tq),
        in_specs=[pl.BlockSpec((1, tq, B_DV), lambda i, h, q: (i, q, h)),
                  pl.BlockSpec((1, l, B_DV), lambda i, h, q: (i, 0, h)),
                  pl.BlockSpec((1, l, B_DV), lambda i, h, q: (i, 0, h)),
                  _const_spec(bl.shape), _const_spec(bn.shape)],
        out_specs=pl.BlockSpec((1, tq, B_DV), lambda i, h, q: (i, q, h)),
        out_shape=jax.ShapeDtypeStruct((b, l, B_WIDTH), F32),
        scratch_shapes=[pltpu.VMEM((l, B_DV), BF16),
                        pltpu.VMEM((l // tq, B_DV + ONES_ROWS, tq), BF16),
                        pltpu.VMEM((2, B_DV + ONES_ROWS, tq), F32)],
        compiler_params=_params("parallel", "parallel", "arbitrary"),
        name="attn_prompt",
    )(bq, bk, bv, bl, bn)


def _attn_sample_kernel(q_ref, kp_ref, vp_ref, kn_ref, vn_ref, bl_ref, bn_ref, o_ref, *, lam_init):
    q1, q2 = _split_maps(q_ref[0])
    kp = kp_ref[0].astype(BF16)
    vp = vp_ref[0].astype(BF16)
    kn = kn_ref[0].astype(BF16)
    vn = vn_ref[0].astype(BF16)

    def one_map(qm):
        sp = _dot_nt(qm, kp)
        sn = _dot_nt(qm, kn)
        m = jnp.maximum(jnp.max(sp, axis=-1, keepdims=True), jnp.max(sn, axis=-1, keepdims=True))
        pp = jnp.exp(sp - m)
        pn = jnp.exp(sn - m)
        l = jnp.sum(pp, axis=-1, keepdims=True) + jnp.sum(pn, axis=-1, keepdims=True)
        a = _dot(pp.astype(BF16), vp) + _dot(pn.astype(BF16), vn)
        return a, l

    a1, l1 = one_map(q1)
    a2, l2 = one_map(q2)
    o_ref[0] = _attn_finish(a1, l1, a2, l2, _lambda(bl_ref, lam_init), bn_ref[...], lam_init).astype(o_ref.dtype)


def _attn_sample(bq, bk, bv, past_k, past_v, layer, bl, bn, lam_init):
    b, l, _ = bq.shape
    lp = past_k.shape[2]
    kern = functools.partial(_attn_sample_kernel, lam_init=lam_init)
    return pl.pallas_call(
        kern,
        grid=(b, B_HEADS),
        in_specs=[pl.BlockSpec((1, l, B_DV), lambda i, h: (i, 0, h)),
                  pl.BlockSpec((None, 1, lp, B_DV), lambda i, h: (layer, i, 0, h)),
                  pl.BlockSpec((None, 1, lp, B_DV), lambda i, h: (layer, i, 0, h)),
                  pl.BlockSpec((1, l, B_DV), lambda i, h: (i, 0, h)),
                  pl.BlockSpec((1, l, B_DV), lambda i, h: (i, 0, h)),
                  _const_spec(bl.shape), _const_spec(bn.shape)],
        out_specs=pl.BlockSpec((1, l, B_DV), lambda i, h: (i, 0, h)),
        out_shape=jax.ShapeDtypeStruct((b, l, B_WIDTH), F32),
        compiler_params=_params("parallel", "parallel"),
        name="attn_sample",
    )(bq, past_k, past_v, bk, bv, bl, bn)


def _pool_kernel(x_ref, h0_ref, cw_ref, cs_ref, o_ref, halo_scr, *, pos0, lt):
    j = pl.program_id(1)

    @pl.when(j == 0)
    def _():
        halo_scr[...] = h0_ref[0]

    x = x_ref[0]
    xp = jnp.concatenate([halo_scr[...], x], axis=0)
    halo_scr[...] = x[lt - HALO:, :]
    pos = pos0 + j * lt + lax.broadcasted_iota(jnp.int32, (lt, 1), 0)
    for gi, win in enumerate(C_WINDOWS):
        sl = slice(gi * C_GROUP_DIM, (gi + 1) * C_GROUP_DIM)
        acc = xp[:, sl]
        s = 1
        while s < win:
            acc = acc + pltpu.roll(acc, s, 0)
            s *= 2
        cnt = jnp.minimum(pos + 1, win).astype(F32)
        pooled = acc[HALO:, :] / cnt - x[:, sl]
        y = _dot(pooled.astype(BF16), cw_ref[gi]) * cs_ref[:, sl]
        o_ref[0, :, sl] = y.astype(o_ref.dtype)


def _pool(cx, halo0, cw, cs, pos0, lt):
    b, l, _ = cx.shape
    kern = functools.partial(_pool_kernel, pos0=pos0, lt=lt)
    return pl.pallas_call(
        kern,
        grid=(b, l // lt),
        in_specs=[pl.BlockSpec((1, lt, C_WIDTH), lambda i, j: (i, j, 0)),
                  pl.BlockSpec((1, HALO, C_WIDTH), lambda i, j: (i, 0, 0)),
                  _const_spec(cw.shape), _const_spec(cs.shape)],
        out_specs=pl.BlockSpec((1, lt, C_WIDTH), lambda i, j: (i, j, 0)),
        out_shape=jax.ShapeDtypeStruct((b, l, C_WIDTH), F32),
        scratch_shapes=[pltpu.VMEM((HALO, C_WIDTH), F32)],
        compiler_params=_params("parallel", "arbitrary"),
        name="pool",
    )(cx, halo0, cw, cs)


def _outffn_kernel(x_ref, oa_ref, ob_ref, oc_ref, wo_ref, gf_ref, up_ref, fw_ref, fb_ref, dn_ref,
                   f0_ref, gfin_ref, y_ref, fnew_ref, halo_scr, *, seg_len, tiles_per_seq, final_norm):
    i = pl.program_id(0)
    tm = x_ref.shape[0]
    nseg = tm // seg_len
    d_ff = dn_ref.shape[0]
    hw = FFN_CONV - 1

    if tiles_per_seq > 1:
        @pl.when(i % tiles_per_seq == 0)
        def _():
            halo_scr[...] = f0_ref[0]

    x1 = x_ref[...]
    x1 = x1 + _dot(oa_ref[...].astype(BF16), wo_ref[0:A_WIDTH, :])
    x1 = x1 + _dot(ob_ref[...].astype(BF16), wo_ref[A_WIDTH:A_WIDTH + B_WIDTH, :])
    x1 = x1 + _dot(oc_ref[...].astype(BF16), wo_ref[A_WIDTH + B_WIDTH:, :])
    h = x1 * lax.rsqrt(jnp.mean(x1 * x1, axis=-1, keepdims=True) + EPS) * gf_ref[...]
    u = _dot(h.astype(BF16), up_ref[...])

    row = lax.broadcasted_iota(jnp.int32, (seg_len, 1), 0)
    acts = []
    for s in range(nseg):
        us = u[s * seg_len:(s + 1) * seg_len, :]
        prev = halo_scr[...] if tiles_per_seq > 1 else f0_ref[s]
        c = fb_ref[...]
        for t in range(FFN_CONV):
            sh = hw - t
            if sh == 0:
                tap = us
            else:
                rolled = pltpu.roll(us, sh, 0)
                for r in range(sh):
                    rolled = jnp.where(row == r, prev[hw - sh + r:hw - sh + r + 1, :], rolled)
                tap = rolled
            c = c + tap * fw_ref[t:t + 1, :]
        acts.append((_silu(c[:, :d_ff]) * c[:, d_ff:]).astype(BF16))
        fnew_ref[s] = us[seg_len - hw:, :]
    if tiles_per_seq > 1:
        halo_scr[...] = u[tm - hw:, :]
    act = acts[0] if nseg == 1 else jnp.concatenate(acts, axis=0)
    x2 = x1 + _dot(act, dn_ref[...])
    if final_norm:
        x2 = x2 * lax.rsqrt(jnp.mean(x2 * x2, axis=-1, keepdims=True) + EPS) * gfin_ref[...]
    y_ref[...] = x2.astype(y_ref.dtype)


def _outffn(x2d, oa, ob, oc, wo, gf, up, fw, fb, dn, f0, gfin, seq_len, tm, final_norm):
    t, d = x2d.shape
    nb = f0.shape[0]
    two_ff = up.shape[1]
    seg_len = min(tm, seq_len)
    tiles_per_seq = max(1, seq_len // tm)
    nseg = tm // seg_len
    hw = FFN_CONV - 1
    kern = functools.partial(_outffn_kernel, seg_len=seg_len, tiles_per_seq=tiles_per_seq,
                             final_norm=final_norm)
    tok = lambda n: pl.BlockSpec((tm, n), lambda i: (i, 0))
    seq_map = lambda i: (i // tiles_per_seq, 0, 0)
    return pl.pallas_call(
        kern,
        grid=(t // tm,),
        in_specs=[tok(d), tok(A_WIDTH), tok(B_WIDTH), tok(C_WIDTH),
                  _const_spec(wo.shape), _const_spec(gf.shape), _const_spec(up.shape),
                  _const_spec(fw.shape), _const_spec(fb.shape), _const_spec(dn.shape),
                  pl.BlockSpec((nseg, hw, two_ff), seq_map),
                  _const_spec(gfin.shape)],
        out_specs=[tok(d), pl.BlockSpec((nseg, hw, two_ff), seq_map)],
        out_shape=[jax.ShapeDtypeStruct((t, d), F32),
                   jax.ShapeDtypeStruct((nb, hw, two_ff), F32)],
        scratch_shapes=[pltpu.VMEM((hw, two_ff), F32)],
        compiler_params=_params("arbitrary"),
        name="outffn",
    )(x2d, oa, ob, oc, wo, gf, up, fw, fb, dn, f0, gfin)


def _pad_rows_top(a, rows):
    return jnp.pad(a, ((0, 0), (rows - a.shape[1], 0), (0, 0)))


def _layer(x, pos0, s0, conv0, past, pool0, ffn0, wts, lam_init, final_norm, gfin):
    (g_mix, w_in, cw, cb, alog, dtb, anorm, bl, bn, c_w, c_s, wo, g_ffn, up, fw, fb, dn) = wts
    b, l, d = x.shape
    t = b * l
    x2d = x.reshape(t, d)
    tm = min(256, t)
    qkv, z, ab, bq, bk, bv, cx = _inproj(x2d, g_mix, w_in, tm)
    r3 = lambda a: a.reshape(b, l, a.shape[-1])
    qkv, z, ab, bq, bk, bv, cx = map(r3, (qkv, z, ab, bq, bk, bv, cx))

    o_a, new_state = _deltanet(qkv, z, ab, cw, cb, alog, dtb, anorm, s0,
                               _pad_rows_top(conv0, SUBLANES), min(l, 512))
    if past is None:
        o_b = _attn_prompt(bq, bk, bv, bl, bn, lam_init, min(l, 512))
    else:
        o_b = _attn_sample(bq, bk, bv, *past, bl, bn, lam_init)
    o_c = _pool(cx, _pad_rows_top(pool0, HALO), c_w, c_s, pos0, min(l, 512))

    y, new_ffn = _outffn(x2d, o_a.reshape(t, -1), o_b.reshape(t, -1), o_c.reshape(t, -1),
                         wo, g_ffn, up, fw, fb, dn, ffn0, gfin, l, tm, final_norm)
    new = (new_state, qkv[:, l - (A_CONV - 1):, :], bk.reshape(b, l, B_HEADS, 2 * B_DQK),
           bv.reshape(b, l, B_HEADS, B_DV), cx[:, l - C_POOL_BUF:, :], new_ffn)
    return y.reshape(b, l, d), new


def _prep_weights(l, norm_mix, w_in, a_conv_w, a_conv_b, a_log, a_dt_bias, a_norm, b_lambda, b_norm,
                  c_w, c_scale, w_out, norm_ffn, ffn_up, ffn_conv_w, ffn_conv_b, ffn_down):
    w = w_in[l]
    n_qkvz = 4 * A_WIDTH
    w_ab = jnp.pad(w[:, n_qkvz:n_qkvz + 2 * A_HEADS], ((0, 0), (0, LANES - 2 * A_HEADS)))
    w_packed = jnp.concatenate([w[:, :n_qkvz], w_ab, w[:, n_qkvz + 2 * A_HEADS:]], axis=1).astype(BF16)
    lane_pad = lambda v: jnp.pad(v.reshape(1, -1), ((0, 0), (0, LANES - v.shape[-1])))
    return (norm_mix[l].reshape(1, -1), w_packed, a_conv_w[l], a_conv_b[l].reshape(1, -1),
            lane_pad(a_log[l]), lane_pad(a_dt_bias[l]), a_norm[l].reshape(1, -1),
            b_lambda[l], b_norm[l].reshape(1, -1), c_w[l].astype(BF16), c_scale[l].reshape(1, -1),
            w_out[l].astype(BF16), norm_ffn[l].reshape(1, -1), ffn_up[l].astype(BF16),
            ffn_conv_w[l], ffn_conv_b[l].reshape(1, -1), ffn_down[l].astype(BF16))


def kernel(x_prompt, x_sample, state_delta, cache_qkv_conv, cache_k, cache_v, cache_pool, cache_ffn_conv,
           norm_mix, w_in, a_conv_w, a_conv_b, a_log, a_dt_bias, a_norm, b_lambda, b_norm, c_w, c_scale,
           w_out, norm_ffn, ffn_up, ffn_conv_w, ffn_conv_b, ffn_down, norm_final):
    depth = w_in.shape[0]
    nbp = x_prompt.shape[0]
    two_ff = ffn_up.shape[-1]
    gfin = norm_final.reshape(1, -1)
    past_k = cache_k.reshape(cache_k.shape[:3] + (B_WIDTH,))
    past_v = cache_v.reshape(cache_v.shape[:3] + (B_WIDTH,))
    yp, ys = x_prompt, x_sample
    p_new, s_new = [], []
    for l in range(depth):
        lam_init = 0.8 - 0.6 * math.exp(-0.3 * l)
        wts = _prep_weights(l, norm_mix, w_in, a_conv_w, a_conv_b, a_log, a_dt_bias, a_norm, b_lambda,
                            b_norm, c_w, c_scale, w_out, norm_ffn, ffn_up, ffn_conv_w, ffn_conv_b, ffn_down)
        final = l == depth - 1
        yp, sp = _layer(
            yp, 0,
            jnp.zeros((nbp, A_HEADS, A_DK, A_DV), F32),
            jnp.zeros((nbp, A_CONV - 1, 3 * A_WIDTH), F32),
            None,
            jnp.zeros((nbp, C_POOL_BUF, C_WIDTH), F32),
            jnp.zeros((nbp, FFN_CONV - 1, two_ff), F32),
            wts, lam_init, final, gfin)
        ys, ss = _layer(
            ys, cache_k.shape[2], state_delta[l], cache_qkv_conv[l], (past_k, past_v, l),
            cache_pool[l], cache_ffn_conv[l], wts, lam_init, final, gfin)
        p_new.append(sp)
        s_new.append(ss)
    p_delta, p_conv, p_k, p_v, p_pool, p_ffn = [jnp.stack(t) for t in zip(*p_new)]
    s_delta, s_conv, s_k, s_v, s_pool, s_ffn = [jnp.stack(t) for t in zip(*s_new)]
    return (yp, ys, p_delta, s_delta, p_conv, s_conv, p_k, s_k, p_v, s_v, p_pool, s_pool, p_ffn, s_ffn)
```

```python
import functools
import math

import jax
import jax.numpy as jnp
from jax import lax
from jax.experimental import pallas as pl
from jax.experimental.pallas import tpu as pltpu

F32 = jnp.float32
BF16 = jnp.bfloat16

CHUNK = 64
EPS = 1e-6
NEG_INF = -1e30
A_HEADS = 4
A_DK = 128
A_DV = 128
A_WIDTH = A_HEADS * A_DV
A_CONV = 4
B_HEADS = 4
B_DQK = 64
B_DV = 128
B_WIDTH = B_HEADS * B_DV
C_WINDOWS = (2, 4, 8, 16)
C_GROUP_DIM = 128
C_WIDTH = 512
C_POOL_BUF = 15
FFN_CONV = 3
LANES = 128
SUBLANES = 8
HALO = 16
VMEM_LIMIT = 56 * 1024 * 1024

PROJ_WIDTHS = (3 * A_WIDTH, A_WIDTH, LANES, B_WIDTH, B_WIDTH, B_WIDTH, C_WIDTH)


def _dot(a, b):
    return jnp.dot(a, b, preferred_element_type=F32)


def _dot_exact(a, b):
    return jnp.dot(a, b, preferred_element_type=F32, precision=lax.Precision.HIGHEST)


def _dot_nt(a, b):
    return lax.dot_general(a, b, (((1,), (1,)), ((), ())), preferred_element_type=F32)


def _dot_tn(a, b):
    return lax.dot_general(a, b, (((0,), (0,)), ((), ())), preferred_element_type=F32)


def _sigmoid(x):
    return 1.0 / (1.0 + jnp.exp(-x))


def _silu(x):
    return x * _sigmoid(x)


def _softplus(x):
    return jnp.maximum(x, 0.0) + jnp.log1p(jnp.exp(-jnp.abs(x)))


def _const_spec(shape):
    nd = len(shape)
    return pl.BlockSpec(shape, lambda *_: (0,) * nd, pipeline_mode=pl.Buffered(1))


def _params(*sem):
    return pltpu.CompilerParams(dimension_semantics=sem, vmem_limit_bytes=VMEM_LIMIT)


KV_SLOTS = (4, 5)


def _inproj_kernel(x_ref, g_ref, w_ref, *refs):
    out_refs = refs[-len(PROJ_WIDTHS):]
    tm = x_ref.shape[0]
    x = x_ref[...]
    h = x * lax.rsqrt(jnp.mean(x * x, axis=-1, keepdims=True) + EPS) * g_ref[...]
    hb = h.astype(BF16)
    off = 0
    for slot, (ref, n) in enumerate(zip(out_refs, PROJ_WIDTHS)):
        y = _dot(hb, w_ref[:, off:off + n])
        if slot in KV_SLOTS:
            for hd in range(B_HEADS):
                ref[pl.ds(hd, tm, stride=B_HEADS), :] = y[:, hd * B_DV:(hd + 1) * B_DV]
        else:
            ref[...] = y.astype(ref.dtype)
        off += n


def _inproj(x2d, g, w, tm, layer, depth, kv_bufs):
    t, d = x2d.shape
    n_all = w.shape[1]
    in_specs = [pl.BlockSpec((tm, d), lambda i: (i, 0)), _const_spec((1, d)), _const_spec((d, n_all))]
    out_specs, out_shape = [], []
    for slot, n in enumerate(PROJ_WIDTHS):
        if slot in KV_SLOTS:
            out_specs.append(pl.BlockSpec((None, tm * B_HEADS, B_DV), lambda i: (layer, i, 0)))
            out_shape.append(jax.ShapeDtypeStruct((depth, t * B_HEADS, B_DV), F32))
        else:
            out_specs.append(pl.BlockSpec((tm, n), lambda i: (i, 0)))
            out_shape.append(jax.ShapeDtypeStruct((t, n), F32))
    args, aliases = [x2d, g, w], {}
    if kv_bufs is not None:
        for buf, slot in zip(kv_bufs, KV_SLOTS):
            aliases[len(args)] = slot
            args.append(buf)
            in_specs.append(pl.BlockSpec(memory_space=pl.ANY))
    return pl.pallas_call(
        _inproj_kernel,
        grid=(t // tm,),
        in_specs=in_specs,
        out_specs=out_specs,
        out_shape=out_shape,
        input_output_aliases=aliases,
        compiler_params=_params("parallel"),
        name="inproj",
    )(*args)


def _split_bf16(x):
    hi = x.astype(BF16)
    lo = (x - hi.astype(F32)).astype(BF16)
    return hi, lo


def _deltanet_kernel(qkv_ref, z_ref, ab_ref, cw_ref, cb_ref, alog_ref, dtb_ref, anorm_ref,
                     s0_ref, c0_ref, o_ref, sout_ref,
                     state_scr, x_scr, u_scr, wq_scr, kt_scr, qk_scr, gl_scr, *, n_chunks):
    j = pl.program_id(1)
    lb = n_chunks * CHUNK
    wide = A_HEADS * CHUNK
    log_chunk = CHUNK.bit_length() - 1

    @pl.when(j == 0)
    def _():
        state_scr[...] = s0_ref[0]
        x_scr[0:SUBLANES, :] = c0_ref[0]

    @pl.when(j > 0)
    def _():
        x_scr[0:SUBLANES, :] = x_scr[lb:lb + SUBLANES, :]

    x_scr[SUBLANES:SUBLANES + lb, :] = qkv_ref[0]

    row_w = lax.broadcasted_iota(jnp.int32, (CHUNK, wide), 0)
    lane_w = lax.broadcasted_iota(jnp.int32, (CHUNK, wide), 1)
    col_w = jnp.bitwise_and(lane_w, CHUNK - 1)
    grp_w = jnp.right_shift(lane_w, log_chunk)
    incl_w = row_w >= col_w
    strict_w = row_w > col_w
    eye_w = row_w == col_w
    row_l = lax.broadcasted_iota(jnp.int32, (CHUNK, LANES), 0)
    bd_mask = (jnp.right_shift(lax.broadcasted_iota(jnp.int32, (wide, wide), 0), log_chunk)
               == jnp.right_shift(lax.broadcasted_iota(jnp.int32, (wide, wide), 1), log_chunk))
    kd_mask = (jnp.right_shift(lax.broadcasted_iota(jnp.int32, (wide, A_WIDTH), 0), log_chunk)
               == jnp.right_shift(lax.broadcasted_iota(jnp.int32, (wide, A_WIDTH), 1), A_DK.bit_length() - 1))
    ones_dk = jnp.ones((A_DK, A_DK), BF16)
    neg_a = -jnp.exp(alog_ref[...])
    dtb = dtb_ref[...]
    anorm = anorm_ref[...]
    cbias = cb_ref[...]
    pad = SUBLANES - (A_CONV - 1)

    def block_diag(xw):
        return jnp.where(bd_mask, jnp.concatenate([xw] * A_HEADS, axis=0), jnp.zeros((), xw.dtype))

    def widen(x, base):
        out = jnp.broadcast_to(x[:, base:base + 1], (CHUNK, wide))
        for h in range(1, A_HEADS):
            out = jnp.where(grp_w == h, x[:, base + h:base + h + 1], out)
        return out

    def mm_wide(x, yw):
        m = x.shape[0]
        xh, xl = _split_bf16(x)
        yh, yl = _split_bf16(yw)
        top = _dot(jnp.concatenate([xh, xl], axis=0), block_diag(yh))
        return top[:m] + top[m:] + _dot(xh, block_diag(yl))

    group = next(g for g in (4, 2, 1) if n_chunks % g == 0)

    def pass1_front(c):
        r0 = pl.multiple_of(c * CHUNK, CHUNK)
        rows = pl.ds(r0, CHUNK)
        xp = x_scr[pl.ds(r0, SUBLANES + CHUNK), :]
        y = cbias
        for i in range(A_CONV):
            back = A_CONV - 1 - i
            tap = xp if back == 0 else pltpu.roll(xp, back, 0)
            y = y + tap[SUBLANES:, :] * cw_ref[i:i + 1, :]
        y = _silu(y)
        yqk = y[:, :2 * A_WIDTH]
        sq = (yqk * yqk).astype(BF16)
        qs, ks = [], []
        for h in range(A_HEADS):
            q = y[:, h * A_DK:(h + 1) * A_DK]
            k = y[:, A_WIDTH + h * A_DK:A_WIDTH + (h + 1) * A_DK]
            qq = _dot(sq[:, h * A_DK:(h + 1) * A_DK], ones_dk)
            kk = _dot(sq[:, A_WIDTH + h * A_DK:A_WIDTH + (h + 1) * A_DK], ones_dk)
            qs.append(q * (lax.rsqrt(qq + EPS) * (A_DK ** -0.5)))
            ks.append(k * lax.rsqrt(kk + EPS))
        qn = jnp.concatenate(qs, axis=-1)
        kn = jnp.concatenate(ks, axis=-1)

        abv = ab_ref[0, rows, :]
        g = neg_a * _softplus(abv + dtb)
        beta = _sigmoid(abv)
        gsum = g
        s = 1
        while s < CHUNK:
            gsum = gsum + jnp.where(row_l >= s, pltpu.roll(gsum, s, 0), 0.0)
            s *= 2
        gc_w = widen(gsum, 0)
        bc_w = widen(beta, A_HEADS)
        gr_w = jnp.sum(jnp.where(eye_w, gc_w, 0.0), axis=0, keepdims=True)
        decay_w = jnp.where(incl_w, jnp.exp(jnp.where(incl_w, gc_w - gr_w, 0.0)), 0.0)

        qb = qn.astype(BF16)
        kb = kn.astype(BF16)
        kdt = jnp.where(kd_mask, jnp.concatenate([kb] * A_HEADS, axis=0), jnp.zeros((), BF16))
        qkk = _dot_nt(jnp.concatenate([qb, kb], axis=0), kdt)
        qk_w = qkk[:CHUNK] * decay_w
        npow = jnp.where(strict_w, -(bc_w * decay_w * qkk[CHUNK:]), 0.0)
        return dict(c=c, rows=rows, y=y, qs=qs, ks=ks, gsum=gsum, beta=beta, qk_w=qk_w, npow=npow)

    def pass1_back(f, tinv):
        c, rows, y, qs, ks, gsum, beta, qk_w = (f[n] for n in ("c", "rows", "y", "qs", "ks", "gsum", "beta", "qk_w"))
        t_low = jnp.where(eye_w, 0.0, tinv)
        g_last = gsum[CHUNK - 1:CHUNK, :]
        gl_scr[c] = jnp.exp(g_last)
        for h in range(A_HEADS):
            hs = slice(h * A_DK, (h + 1) * A_DK)
            cs = slice(h * CHUNK, (h + 1) * CHUNK)
            v = y[:, 2 * A_WIDTH + h * A_DV:2 * A_WIDTH + (h + 1) * A_DV]
            gc = gsum[:, h:h + 1]
            bc = beta[:, A_HEADS + h:A_HEADS + h + 1]
            gam = jnp.exp(gc)
            rhs = jnp.concatenate([v * bc, ks[h] * (bc * gam)], axis=-1)
            uw = rhs + _dot(t_low[:, cs].astype(BF16), rhs.astype(BF16))
            qk_scr[c, h] = qk_w[:, cs].astype(BF16)
            u_scr[rows, hs] = uw[:, :A_DV]
            wq_scr[c, h] = jnp.concatenate([uw[:, A_DV:], qs[h] * gam], axis=0).astype(BF16)
            kt_scr[rows, hs] = (ks[h] * jnp.exp(g_last[:, h:h + 1] - gc)).astype(BF16)

    def pass1(i, carry):
        fronts = [pass1_front(i * group + t) for t in range(group)]
        npows = [f["npow"] for f in fronts]
        tinvs = [jnp.where(eye_w, 1.0, n) for n in npows]
        npows = [mm_wide(n, n) for n in npows]
        p = 2
        while 2 * p < CHUNK:
            boths = [mm_wide(jnp.concatenate([n, t], axis=0), n) for n, t in zip(npows, tinvs)]
            npows = [b[:CHUNK] for b in boths]
            tinvs = [t + b[CHUNK:] for t, b in zip(tinvs, boths)]
            p *= 2
        tinvs = [t + mm_wide(t, n) for n, t in zip(npows, tinvs)]
        for f, t in zip(fronts, tinvs):
            pass1_back(f, t)
        return carry

    def pass2(c, carry):
        r0 = pl.multiple_of(c * CHUNK, CHUNK)
        rows = pl.ds(r0, CHUNK)
        zv = z_ref[0, rows, :]
        glv = gl_scr[c]
        heads = range(A_HEADS)
        hsl = [slice(h * A_DK, (h + 1) * A_DK) for h in heads]
        sts = [state_scr[h] for h in heads]
        wss = [_dot(wq_scr[c, h], sts[h].astype(BF16)) for h in heads]
        dbs = [(u_scr[rows, hsl[h]] - wss[h][:CHUNK]).astype(BF16) for h in heads]
        outs = [wss[h][CHUNK:] + _dot(qk_scr[c, h], dbs[h]) for h in heads]
        for h in heads:
            state_scr[h] = sts[h] * glv[:, h:h + 1] + _dot_tn(kt_scr[rows, hsl[h]], dbs[h])
        for h in heads:
            o = outs[h]
            on = o * lax.rsqrt(jnp.mean(o * o, axis=-1, keepdims=True) + EPS) * anorm
            gate = _silu(zv[:, hsl[h]])
            o_ref[0, rows, hsl[h]] = (on * gate).astype(o_ref.dtype)
        return carry

    lax.fori_loop(0, n_chunks // group, pass1, 0)
    lax.fori_loop(0, n_chunks, pass2, 0)

    @pl.when(j == pl.num_programs(1) - 1)
    def _():
        sout_ref[0] = state_scr[...]


def _deltanet(qkv, z, ab, cw, cb, alog, dtb, anorm, s0, c0, lb):
    b, l, _ = qkv.shape
    n_chunks = lb // CHUNK
    kern = functools.partial(_deltanet_kernel, n_chunks=n_chunks)
    return pl.pallas_call(
        kern,
        grid=(b, l // lb),
        in_specs=[pl.BlockSpec((1, lb, 3 * A_WIDTH), lambda i, j: (i, j, 0)),
                  pl.BlockSpec((1, lb, A_WIDTH), lambda i, j: (i, j, 0)),
                  pl.BlockSpec((1, lb, LANES), lambda i, j: (i, j, 0)),
                  _const_spec(cw.shape), _const_spec(cb.shape), _const_spec(alog.shape),
                  _const_spec(dtb.shape), _const_spec(anorm.shape),
                  pl.BlockSpec((1, A_HEADS, A_DK, A_DV), lambda i, j: (i, 0, 0, 0)),
                  pl.BlockSpec((1, SUBLANES, 3 * A_WIDTH), lambda i, j: (i, 0, 0))],
        out_specs=[pl.BlockSpec((1, lb, A_WIDTH), lambda i, j: (i, j, 0)),
                   pl.BlockSpec((1, A_HEADS, A_DK, A_DV), lambda i, j: (i, 0, 0, 0))],
        out_shape=[jax.ShapeDtypeStruct((b, l, A_WIDTH), F32),
                   jax.ShapeDtypeStruct((b, A_HEADS, A_DK, A_DV), F32)],
        scratch_shapes=[pltpu.VMEM((A_HEADS, A_DK, A_DV), F32),
                        pltpu.VMEM((SUBLANES + lb, 3 * A_WIDTH), F32),
                        pltpu.VMEM((lb, A_WIDTH), F32),
                        pltpu.VMEM((n_chunks, A_HEADS, 2 * CHUNK, A_DK), BF16),
                        pltpu.VMEM((lb, A_WIDTH), BF16),
                        pltpu.VMEM((n_chunks, A_HEADS, CHUNK, CHUNK), BF16),
                        pltpu.VMEM((n_chunks, 1, LANES), F32)],
        compiler_params=_params("parallel", "arbitrary"),
        name="deltanet",
    )(qkv, z, ab, cw, cb, alog, dtb, anorm, s0, c0)


def _lambda(bl_ref, lam_init):
    bl = bl_ref[...]
    s01 = jnp.sum(bl[0:1] * bl[1:2], axis=-1, keepdims=True)
    s23 = jnp.sum(bl[2:3] * bl[3:4], axis=-1, keepdims=True)
    return jnp.exp(s01) - jnp.exp(s23) + lam_init


def _split_maps(q):
    lane = lax.broadcasted_iota(jnp.int32, q.shape, 1)
    qs = q * (B_DQK ** -0.5)
    q1 = jnp.where(lane < B_DQK, qs, 0.0).astype(BF16)
    q2 = jnp.where(lane >= B_DQK, qs, 0.0).astype(BF16)
    return q1, q2


def _attn_finish(a1, l1, a2, l2, lam, bn, lam_init):
    o = a1 / l1 - lam * (a2 / l2)
    on = o * lax.rsqrt(jnp.mean(o * o, axis=-1, keepdims=True) + EPS) * bn
    return on * (1.0 - lam_init)


ONES_ROWS = 16


ATTN_HEADS_PER_STEP = 2


def _attn_prompt_kernel(q_ref, k_ref, v_ref, bl_ref, bn_ref, o_ref, kb_scr, vt_scr, acc_scr, *, lam_init, tq):
    nh = ATTN_HEADS_PER_STEP
    hd0 = pl.program_id(1) * nh
    qi = pl.program_id(2)
    seq = kb_scr.shape[1]
    nblk = seq // tq

    @pl.when(qi == 0)
    def _():
        ones = jnp.ones((ONES_ROWS, tq), BF16)
        for hh in range(nh):
            kb_scr[hh] = k_ref[pl.ds(hd0 + hh, seq, stride=B_HEADS), :].astype(BF16)
            for i in range(nblk):
                vt = v_ref[pl.ds(hd0 + hh + i * tq * B_HEADS, tq, stride=B_HEADS), :].T.astype(BF16)
                vt_scr[hh, i] = jnp.concatenate([vt, ones], axis=0)

    qmaps = []
    for hh in range(nh):
        qmaps.extend(_split_maps(q_ref[0, :, hh * B_DV:(hh + 1) * B_DV]))
    key = lax.broadcasted_iota(jnp.int32, (tq, tq), 0)
    qry = lax.broadcasted_iota(jnp.int32, (tq, tq), 1)
    shift = CHUNK.bit_length() - 1
    visible = jnp.right_shift(key, shift) <= jnp.right_shift(qry, shift)
    chains = range(2 * nh)

    def step(kb, ms, masked):
        k0 = pl.multiple_of(kb * tq, tq)
        kblks = [kb_scr[hh, pl.ds(k0, tq), :] for hh in range(nh)]
        vts = [vt_scr[hh, kb] for hh in range(nh)]
        sts = [_dot_nt(kblks[t // 2], qmaps[t]) for t in chains]
        if masked:
            sts = [jnp.where(visible, s, NEG_INF) for s in sts]
        mns = [jnp.maximum(m, jnp.max(s, axis=0, keepdims=True)) for m, s in zip(ms, sts)]
        pts = [jnp.exp(s - mn).astype(BF16) for s, mn in zip(sts, mns)]
        upd = [_dot(vts[t // 2], pts[t]) for t in chains]
        for t in chains:
            acc_scr[t] = jnp.exp(ms[t] - mns[t]) * acc_scr[t] + upd[t]
        return tuple(mns)

    acc_scr[...] = jnp.zeros_like(acc_scr)
    m0 = jnp.full((1, tq), NEG_INF, F32)
    ms = lax.fori_loop(0, qi, lambda kb, c: step(kb, c, False), (m0,) * (2 * nh))
    step(qi, ms, True)
    lam = _lambda(bl_ref, lam_init)
    for hh in range(nh):
        a1, a2 = acc_scr[2 * hh], acc_scr[2 * hh + 1]
        ot = a1[:B_DV] / a1[B_DV:B_DV + 1] - lam * (a2[:B_DV] / a2[B_DV:B_DV + 1])
        ont = ot * lax.rsqrt(jnp.mean(ot * ot, axis=0, keepdims=True) + EPS)
        o_ref[0, :, hh * B_DV:(hh + 1) * B_DV] = (ont.T * bn_ref[...] * (1.0 - lam_init)).astype(o_ref.dtype)


def _kv_spec(rows, layer):
    return pl.BlockSpec((None, None, rows, B_DV), lambda i, *_: (layer, i, 0, 0))


def _attn_prompt(bq, kbuf, vbuf, layer, bl, bn, lam_init, tq):
    b, l, _ = bq.shape
    nh = ATTN_HEADS_PER_STEP
    kern = functools.partial(_attn_prompt_kernel, lam_init=lam_init, tq=tq)
    return pl.pallas_call(
        kern,
        grid=(b, B_HEADS // nh, l // tq),
        in_specs=[pl.BlockSpec((1, tq, nh * B_DV), lambda i, h, q: (i, q, h)),
                  _kv_spec(l * B_HEADS, layer), _kv_spec(l * B_HEADS, layer),
                  _const_spec(bl.shape), _const_spec(bn.shape)],
        out_specs=pl.BlockSpec((1, tq, nh * B_DV), lambda i, h, q: (i, q, h)),
        out_shape=jax.ShapeDtypeStruct((b, l, B_WIDTH), F32),
        scratch_shapes=[pltpu.VMEM((nh, l, B_DV), BF16),
                        pltpu.VMEM((nh, l // tq, B_DV + ONES_ROWS, tq), BF16),
                        pltpu.VMEM((2 * nh, B_DV + ONES_ROWS, tq), F32)],
        compiler_params=_params("parallel", "parallel", "arbitrary"),
        name="attn_prompt",
    )(bq, kbuf, vbuf, bl, bn)


def _attn_sample_kernel(q_ref, kp_ref, vp_ref, kn_ref, vn_ref, bl_ref, bn_ref, o_ref, *, lam_init):
    hd = pl.program_id(1)
    q1, q2 = _split_maps(q_ref[0])

    def head_rows(ref):
        return ref[pl.ds(hd, ref.shape[0] // B_HEADS, stride=B_HEADS), :].astype(BF16)

    kp, vp, kn, vn = head_rows(kp_ref), head_rows(vp_ref), head_rows(kn_ref), head_rows(vn_ref)

    def one_map(qm):
        sp = _dot_nt(qm, kp)
        sn = _dot_nt(qm, kn)
        m = jnp.maximum(jnp.max(sp, axis=-1, keepdims=True), jnp.max(sn, axis=-1, keepdims=True))
        pp = jnp.exp(sp - m)
        pn = jnp.exp(sn - m)
        l = jnp.sum(pp, axis=-1, keepdims=True) + jnp.sum(pn, axis=-1, keepdims=True)
        a = _dot(pp.astype(BF16), vp) + _dot(pn.astype(BF16), vn)
        return a, l

    a1, l1 = one_map(q1)
    a2, l2 = one_map(q2)
    o_ref[0] = _attn_finish(a1, l1, a2, l2, _lambda(bl_ref, lam_init), bn_ref[...], lam_init).astype(o_ref.dtype)


def _attn_sample(bq, kbuf, vbuf, past_k, past_v, layer, bl, bn, lam_init):
    b, l, _ = bq.shape
    kern = functools.partial(_attn_sample_kernel, lam_init=lam_init)
    return pl.pallas_call(
        kern,
        grid=(b, B_HEADS),
        in_specs=[pl.BlockSpec((1, l, B_DV), lambda i, h: (i, 0, h)),
                  _kv_spec(past_k.shape[2], layer), _kv_spec(past_v.shape[2], layer),
                  _kv_spec(l * B_HEADS, layer), _kv_spec(l * B_HEADS, layer),
                  _const_spec(bl.shape), _const_spec(bn.shape)],
        out_specs=pl.BlockSpec((1, l, B_DV), lambda i, h: (i, 0, h)),
        out_shape=jax.ShapeDtypeStruct((b, l, B_WIDTH), F32),
        compiler_params=_params("parallel", "parallel"),
        name="attn_sample",
    )(bq, past_k, past_v, kbuf, vbuf, bl, bn)


def _pool_kernel(x_ref, h0_ref, cw_ref, cs_ref, o_ref, halo_scr, *, pos0, lt):
    j = pl.program_id(1)

    @pl.when(j == 0)
    def _():
        halo_scr[...] = h0_ref[0]

    x = x_ref[0]
    xp = jnp.concatenate([halo_scr[...], x], axis=0)
    halo_scr[...] = x[lt - HALO:, :]
    pos = pos0 + j * lt + lax.broadcasted_iota(jnp.int32, (lt, 1), 0)
    for gi, win in enumerate(C_WINDOWS):
        sl = slice(gi * C_GROUP_DIM, (gi + 1) * C_GROUP_DIM)
        acc = xp[:, sl]
        s = 1
        while s < win:
            acc = acc + pltpu.roll(acc, s, 0)
            s *= 2
        cnt = jnp.minimum(pos + 1, win).astype(F32)
        pooled = acc[HALO:, :] / cnt - x[:, sl]
        y = _dot(pooled.astype(BF16), cw_ref[gi]) * cs_ref[:, sl]
        o_ref[0, :, sl] = y.astype(o_ref.dtype)


def _pool(cx, halo0, cw, cs, pos0, lt):
    b, l, _ = cx.shape
    kern = functools.partial(_pool_kernel, pos0=pos0, lt=lt)
    return pl.pallas_call(
        kern,
        grid=(b, l // lt),
        in_specs=[pl.BlockSpec((1, lt, C_WIDTH), lambda i, j: (i, j, 0)),
                  pl.BlockSpec((1, HALO, C_WIDTH), lambda i, j: (i, 0, 0)),
                  _const_spec(cw.shape), _const_spec(cs.shape)],
        out_specs=pl.BlockSpec((1, lt, C_WIDTH), lambda i, j: (i, j, 0)),
        out_shape=jax.ShapeDtypeStruct((b, l, C_WIDTH), F32),
        scratch_shapes=[pltpu.VMEM((HALO, C_WIDTH), F32)],
        compiler_params=_params("parallel", "arbitrary"),
        name="pool",
    )(cx, halo0, cw, cs)


def _outffn_kernel(x_ref, oa_ref, ob_ref, oc_ref, wo_ref, gf_ref, up_ref, fw_ref, fb_ref, dn_ref,
                   f0_ref, gfin_ref, y_ref, fnew_ref, halo_scr, *, seg_len, tiles_per_seq, final_norm):
    i = pl.program_id(0)
    tm = x_ref.shape[0]
    nseg = tm // seg_len
    d_ff = dn_ref.shape[0]
    hw = FFN_CONV - 1

    if tiles_per_seq > 1:
        @pl.when(i % tiles_per_seq == 0)
        def _():
            halo_scr[...] = f0_ref[0]

    x1 = x_ref[...]
    x1 = x1 + _dot(oa_ref[...].astype(BF16), wo_ref[0:A_WIDTH, :])
    x1 = x1 + _dot(ob_ref[...].astype(BF16), wo_ref[A_WIDTH:A_WIDTH + B_WIDTH, :])
    x1 = x1 + _dot(oc_ref[...].astype(BF16), wo_ref[A_WIDTH + B_WIDTH:, :])
    h = x1 * lax.rsqrt(jnp.mean(x1 * x1, axis=-1, keepdims=True) + EPS) * gf_ref[...]
    u = _dot(h.astype(BF16), up_ref[...])

    row = lax.broadcasted_iota(jnp.int32, (seg_len, 1), 0)
    acts = []
    for s in range(nseg):
        us = u[s * seg_len:(s + 1) * seg_len, :]
        prev = halo_scr[...] if tiles_per_seq > 1 else f0_ref[s]
        c = fb_ref[...]
        for t in range(FFN_CONV):
            sh = hw - t
            if sh == 0:
                tap = us
            else:
                rolled = pltpu.roll(us, sh, 0)
                for r in range(sh):
                    rolled = jnp.where(row == r, prev[hw - sh + r:hw - sh + r + 1, :], rolled)
                tap = rolled
            c = c + tap * fw_ref[t:t + 1, :]
        acts.append((_silu(c[:, :d_ff]) * c[:, d_ff:]).astype(BF16))
        fnew_ref[s] = us[seg_len - hw:, :]
    if tiles_per_seq > 1:
        halo_scr[...] = u[tm - hw:, :]
    act = acts[0] if nseg == 1 else jnp.concatenate(acts, axis=0)
    x2 = x1 + _dot(act, dn_ref[...])
    if final_norm:
        x2 = x2 * lax.rsqrt(jnp.mean(x2 * x2, axis=-1, keepdims=True) + EPS) * gfin_ref[...]
    y_ref[...] = x2.astype(y_ref.dtype)


def _outffn(x2d, oa, ob, oc, wo, gf, up, fw, fb, dn, f0, gfin, seq_len, tm, final_norm):
    t, d = x2d.shape
    nb = f0.shape[0]
    two_ff = up.shape[1]
    seg_len = min(tm, seq_len)
    tiles_per_seq = max(1, seq_len // tm)
    nseg = tm // seg_len
    hw = FFN_CONV - 1
    kern = functools.partial(_outffn_kernel, seg_len=seg_len, tiles_per_seq=tiles_per_seq,
                             final_norm=final_norm)
    tok = lambda n: pl.BlockSpec((tm, n), lambda i: (i, 0))
    seq_map = lambda i: (i // tiles_per_seq, 0, 0)
    return pl.pallas_call(
        kern,
        grid=(t // tm,),
        in_specs=[tok(d), tok(A_WIDTH), tok(B_WIDTH), tok(C_WIDTH),
                  _const_spec(wo.shape), _const_spec(gf.shape), _const_spec(up.shape),
                  _const_spec(fw.shape), _const_spec(fb.shape), _const_spec(dn.shape),
                  pl.BlockSpec((nseg, hw, two_ff), seq_map),
                  _const_spec(gfin.shape)],
        out_specs=[tok(d), pl.BlockSpec((nseg, hw, two_ff), seq_map)],
        out_shape=[jax.ShapeDtypeStruct((t, d), F32),
                   jax.ShapeDtypeStruct((nb, hw, two_ff), F32)],
        scratch_shapes=[pltpu.VMEM((hw, two_ff), F32)],
        compiler_params=_params("arbitrary"),
        name="outffn",
    )(x2d, oa, ob, oc, wo, gf, up, fw, fb, dn, f0, gfin)


def _pad_rows_top(a, rows):
    return jnp.pad(a, ((0, 0), (rows - a.shape[1], 0), (0, 0)))


def _layer(x, layer, depth, kv_bufs, pos0, s0, conv0, past, pool0, ffn0, wts, lam_init, final_norm, gfin):
    (g_mix, w_in, cw, cb, alog, dtb, anorm, bl, bn, c_w, c_s, wo, g_ffn, up, fw, fb, dn) = wts
    b, l, d = x.shape
    t = b * l
    x2d = x.reshape(t, d)
    tm = min(256, t)
    qkv, z, ab, bq, kbuf, vbuf, cx = _inproj(x2d, g_mix, w_in, tm, layer, depth, kv_bufs)
    r3 = lambda a: a.reshape(b, l, a.shape[-1])
    qkv, z, ab, bq, cx = map(r3, (qkv, z, ab, bq, cx))
    kv4 = lambda a: a.reshape(depth, b, l * B_HEADS, B_DV)

    o_a, new_state = _deltanet(qkv, z, ab, cw, cb, alog, dtb, anorm, s0,
                               _pad_rows_top(conv0, SUBLANES), min(l, 512))
    if past is None:
        o_b = _attn_prompt(bq, kv4(kbuf), kv4(vbuf), layer, bl, bn, lam_init, min(l, 512))
    else:
        o_b = _attn_sample(bq, kv4(kbuf), kv4(vbuf), *past, layer, bl, bn, lam_init)
    o_c = _pool(cx, _pad_rows_top(pool0, HALO), c_w, c_s, pos0, min(l, 512))

    y, new_ffn = _outffn(x2d, o_a.reshape(t, -1), o_b.reshape(t, -1), o_c.reshape(t, -1),
                         wo, g_ffn, up, fw, fb, dn, ffn0, gfin, l, tm, final_norm)
    new = (new_state, qkv[:, l - (A_CONV - 1):, :], cx[:, l - C_POOL_BUF:, :], new_ffn)
    return y.reshape(b, l, d), (kbuf, vbuf), new


def _prep_weights(l, norm_mix, w_in, a_conv_w, a_conv_b, a_log, a_dt_bias, a_norm, b_lambda, b_norm,
                  c_w, c_scale, w_out, norm_ffn, ffn_up, ffn_conv_w, ffn_conv_b, ffn_down):
    w = w_in[l]
    n_qkvz = 4 * A_WIDTH
    w_ab = jnp.pad(w[:, n_qkvz:n_qkvz + 2 * A_HEADS], ((0, 0), (0, LANES - 2 * A_HEADS)))
    w_packed = jnp.concatenate([w[:, :n_qkvz], w_ab, w[:, n_qkvz + 2 * A_HEADS:]], axis=1).astype(BF16)
    lane_pad = lambda v: jnp.pad(v.reshape(1, -1), ((0, 0), (0, LANES - v.shape[-1])))
    return (norm_mix[l].reshape(1, -1), w_packed, a_conv_w[l], a_conv_b[l].reshape(1, -1),
            lane_pad(a_log[l]), lane_pad(a_dt_bias[l]), a_norm[l].reshape(1, -1),
            b_lambda[l], b_norm[l].reshape(1, -1), c_w[l].astype(BF16), c_scale[l].reshape(1, -1),
            w_out[l].astype(BF16), norm_ffn[l].reshape(1, -1), ffn_up[l].astype(BF16),
            ffn_conv_w[l], ffn_conv_b[l].reshape(1, -1), ffn_down[l].astype(BF16))


def kernel(x_prompt, x_sample, state_delta, cache_qkv_conv, cache_k, cache_v, cache_pool, cache_ffn_conv,
           norm_mix, w_in, a_conv_w, a_conv_b, a_log, a_dt_bias, a_norm, b_lambda, b_norm, c_w, c_scale,
           w_out, norm_ffn, ffn_up, ffn_conv_w, ffn_conv_b, ffn_down, norm_final):
    depth = w_in.shape[0]
    nbp = x_prompt.shape[0]
    two_ff = ffn_up.shape[-1]
    gfin = norm_final.reshape(1, -1)
    rows_form = lambda c: c.reshape(c.shape[:2] + (c.shape[2] * c.shape[3], c.shape[4]))
    past = (rows_form(cache_k), rows_form(cache_v))
    yp, ys = x_prompt, x_sample
    p_kv, s_kv = None, None
    p_new, s_new = [], []
    for l in range(depth):
        lam_init = 0.8 - 0.6 * math.exp(-0.3 * l)
        wts = _prep_weights(l, norm_mix, w_in, a_conv_w, a_conv_b, a_log, a_dt_bias, a_norm, b_lambda,
                            b_norm, c_w, c_scale, w_out, norm_ffn, ffn_up, ffn_conv_w, ffn_conv_b, ffn_down)
        final = l == depth - 1
        yp, p_kv, sp = _layer(
            yp, l, depth, p_kv, 0,
            jnp.zeros((nbp, A_HEADS, A_DK, A_DV), F32),
            jnp.zeros((nbp, A_CONV - 1, 3 * A_WIDTH), F32),
            None,
            jnp.zeros((nbp, C_POOL_BUF, C_WIDTH), F32),
            jnp.zeros((nbp, FFN_CONV - 1, two_ff), F32),
            wts, lam_init, final, gfin)
        ys, s_kv, ss = _layer(
            ys, l, depth, s_kv, cache_k.shape[2], state_delta[l], cache_qkv_conv[l], past,
            cache_pool[l], cache_ffn_conv[l], wts, lam_init, final, gfin)
        p_new.append(sp)
        s_new.append(ss)
    p_delta, p_conv, p_pool, p_ffn = [jnp.stack(t) for t in zip(*p_new)]
    s_delta, s_conv, s_pool, s_ffn = [jnp.stack(t) for t in zip(*s_new)]
    heads_form = lambda buf, x: buf.reshape(depth, x.shape[0], x.shape[1], B_HEADS, B_DV)
    p_k, p_v = (heads_form(buf, x_prompt) for buf in p_kv)
    s_k, s_v = (heads_form(buf, x_sample) for buf in s_kv)
    return (yp, ys, p_delta, s_delta, p_conv, s_conv, p_k, s_k, p_v, s_v, p_pool, s_pool, p_ffn, s_ffn)
```

```python
import functools
import math

import jax
import jax.numpy as jnp
from jax import lax
from jax.experimental import pallas as pl
from jax.experimental.pallas import tpu as pltpu

F32 = jnp.float32
BF16 = jnp.bfloat16

CHUNK = 64
EPS = 1e-6
NEG_INF = -1e30
A_HEADS = 4
A_DK = 128
A_DV = 128
A_WIDTH = A_HEADS * A_DV
A_CONV = 4
B_HEADS = 4
B_DQK = 64
B_DV = 128
B_WIDTH = B_HEADS * B_DV
C_WINDOWS = (2, 4, 8, 16)
C_GROUP_DIM = 128
C_WIDTH = 512
C_POOL_BUF = 15
FFN_CONV = 3
LANES = 128
SUBLANES = 8
HALO = 16
VMEM_LIMIT = 56 * 1024 * 1024

PROJ_WIDTHS = (3 * A_WIDTH, A_WIDTH, LANES, B_WIDTH, B_WIDTH, B_WIDTH, C_WIDTH)


def _dot(a, b):
    return jnp.dot(a, b, preferred_element_type=F32)


def _dot_exact(a, b):
    return jnp.dot(a, b, preferred_element_type=F32, precision=lax.Precision.HIGHEST)


def _dot_nt(a, b):
    return lax.dot_general(a, b, (((1,), (1,)), ((), ())), preferred_element_type=F32)


def _dot_tn(a, b):
    return lax.dot_general(a, b, (((0,), (0,)), ((), ())), preferred_element_type=F32)


def _sigmoid(x):
    return 0.5 * (1.0 + jnp.tanh(0.5 * x))


def _silu(x):
    return x * _sigmoid(x)


def _softplus(x):
    return jnp.maximum(x, 0.0) + jnp.log1p(jnp.exp(-jnp.abs(x)))


def _const_spec(shape):
    nd = len(shape)
    return pl.BlockSpec(shape, lambda *_: (0,) * nd, pipeline_mode=pl.Buffered(1))


def _params(*sem):
    return pltpu.CompilerParams(dimension_semantics=sem, vmem_limit_bytes=VMEM_LIMIT)


KV_SLOTS = (4, 5)


def _inproj_kernel(x_ref, g_ref, w_ref, *refs):
    out_refs = refs[-len(PROJ_WIDTHS):]
    tm = x_ref.shape[0]
    x = x_ref[...]
    h = x * lax.rsqrt(jnp.mean(x * x, axis=-1, keepdims=True) + EPS) * g_ref[...]
    hb = h.astype(BF16)
    off = 0
    for slot, (ref, n) in enumerate(zip(out_refs, PROJ_WIDTHS)):
        y = _dot(hb, w_ref[:, off:off + n])
        if slot in KV_SLOTS:
            for hd in range(B_HEADS):
                ref[pl.ds(hd, tm, stride=B_HEADS), :] = y[:, hd * B_DV:(hd + 1) * B_DV]
        else:
            ref[...] = y.astype(ref.dtype)
        off += n


def _inproj(x2d, g, w, tm, layer, depth, kv_bufs):
    t, d = x2d.shape
    n_all = w.shape[1]
    in_specs = [pl.BlockSpec((tm, d), lambda i: (i, 0)), _const_spec((1, d)), _const_spec((d, n_all))]
    out_specs, out_shape = [], []
    for slot, n in enumerate(PROJ_WIDTHS):
        if slot in KV_SLOTS:
            out_specs.append(pl.BlockSpec((None, tm * B_HEADS, B_DV), lambda i: (layer, i, 0)))
            out_shape.append(jax.ShapeDtypeStruct((depth, t * B_HEADS, B_DV), F32))
        else:
            out_specs.append(pl.BlockSpec((tm, n), lambda i: (i, 0)))
            out_shape.append(jax.ShapeDtypeStruct((t, n), F32))
    args, aliases = [x2d, g, w], {}
    if kv_bufs is not None:
        for buf, slot in zip(kv_bufs, KV_SLOTS):
            aliases[len(args)] = slot
            args.append(buf)
            in_specs.append(pl.BlockSpec(memory_space=pl.ANY))
    return pl.pallas_call(
        _inproj_kernel,
        grid=(t // tm,),
        in_specs=in_specs,
        out_specs=out_specs,
        out_shape=out_shape,
        input_output_aliases=aliases,
        compiler_params=_params("parallel"),
        name="inproj",
    )(*args)


def _split_bf16(x):
    hi = x.astype(BF16)
    lo = (x - hi.astype(F32)).astype(BF16)
    return hi, lo


def _deltanet_kernel(qkv_ref, z_ref, ab_ref, cw_ref, cb_ref, alog_ref, dtb_ref, anorm_ref,
                     s0_ref, c0_ref, o_ref, sout_ref,
                     state_scr, x_scr, u_scr, wq_scr, kt_scr, qk_scr, gl_scr, *, n_chunks):
    j = pl.program_id(1)
    lb = n_chunks * CHUNK
    wide = A_HEADS * CHUNK
    log_chunk = CHUNK.bit_length() - 1

    @pl.when(j == 0)
    def _():
        state_scr[...] = s0_ref[0]
        x_scr[0:SUBLANES, :] = c0_ref[0]

    @pl.when(j > 0)
    def _():
        x_scr[0:SUBLANES, :] = x_scr[lb:lb + SUBLANES, :]

    x_scr[SUBLANES:SUBLANES + lb, :] = qkv_ref[0]

    row_w = lax.broadcasted_iota(jnp.int32, (CHUNK, wide), 0)
    lane_w = lax.broadcasted_iota(jnp.int32, (CHUNK, wide), 1)
    col_w = jnp.bitwise_and(lane_w, CHUNK - 1)
    grp_w = jnp.right_shift(lane_w, log_chunk)
    incl_w = row_w >= col_w
    strict_w = row_w > col_w
    eye_w = row_w == col_w
    row_l = lax.broadcasted_iota(jnp.int32, (CHUNK, LANES), 0)
    bd_mask = (jnp.right_shift(lax.broadcasted_iota(jnp.int32, (wide, wide), 0), log_chunk)
               == jnp.right_shift(lax.broadcasted_iota(jnp.int32, (wide, wide), 1), log_chunk))
    kd_mask = (jnp.right_shift(lax.broadcasted_iota(jnp.int32, (wide, A_WIDTH), 0), log_chunk)
               == jnp.right_shift(lax.broadcasted_iota(jnp.int32, (wide, A_WIDTH), 1), A_DK.bit_length() - 1))
    ones_dk = jnp.ones((A_DK, A_DK), BF16)
    neg_a = -jnp.exp(alog_ref[...])
    dtb = dtb_ref[...]
    anorm = anorm_ref[...]
    cbias = cb_ref[...]
    pad = SUBLANES - (A_CONV - 1)

    def block_diag(xw):
        return jnp.where(bd_mask, jnp.concatenate([xw] * A_HEADS, axis=0), jnp.zeros((), xw.dtype))

    def widen(x, base):
        out = jnp.broadcast_to(x[:, base:base + 1], (CHUNK, wide))
        for h in range(1, A_HEADS):
            out = jnp.where(grp_w == h, x[:, base + h:base + h + 1], out)
        return out

    def mm_wide(x, yw):
        m = x.shape[0]
        xh, xl = _split_bf16(x)
        yh, yl = _split_bf16(yw)
        top = _dot(jnp.concatenate([xh, xl], axis=0), block_diag(yh))
        return top[:m] + top[m:] + _dot(xh, block_diag(yl))

    group = next(g for g in (4, 2, 1) if n_chunks % g == 0)

    def pass1_front(c):
        r0 = c * CHUNK if isinstance(c, int) else pl.multiple_of(c * CHUNK, CHUNK)
        rows = pl.ds(r0, CHUNK)
        xp = x_scr[pl.ds(r0, SUBLANES + CHUNK), :]
        y = cbias
        for i in range(A_CONV):
            back = A_CONV - 1 - i
            tap = xp if back == 0 else pltpu.roll(xp, back, 0)
            y = y + tap[SUBLANES:, :] * cw_ref[i:i + 1, :]
        y = _silu(y)
        yqk = y[:, :2 * A_WIDTH]
        sq = (yqk * yqk).astype(BF16)
        qs, ks = [], []
        for h in range(A_HEADS):
            q = y[:, h * A_DK:(h + 1) * A_DK]
            k = y[:, A_WIDTH + h * A_DK:A_WIDTH + (h + 1) * A_DK]
            qq = _dot(sq[:, h * A_DK:(h + 1) * A_DK], ones_dk)
            kk = _dot(sq[:, A_WIDTH + h * A_DK:A_WIDTH + (h + 1) * A_DK], ones_dk)
            qs.append(q * (lax.rsqrt(qq + EPS) * (A_DK ** -0.5)))
            ks.append(k * lax.rsqrt(kk + EPS))
        qn = jnp.concatenate(qs, axis=-1)
        kn = jnp.concatenate(ks, axis=-1)

        abv = ab_ref[0, rows, :]
        g = neg_a * _softplus(abv + dtb)
        beta = _sigmoid(abv)
        gsum = g
        s = 1
        while s < CHUNK:
            gsum = gsum + jnp.where(row_l >= s, pltpu.roll(gsum, s, 0), 0.0)
            s *= 2
        gc_w = widen(gsum, 0)
        bc_w = widen(beta, A_HEADS)
        gr_w = jnp.sum(jnp.where(eye_w, gc_w, 0.0), axis=0, keepdims=True)
        decay_w = jnp.where(incl_w, jnp.exp(jnp.where(incl_w, gc_w - gr_w, 0.0)), 0.0)

        qb = qn.astype(BF16)
        kb = kn.astype(BF16)
        kdt = jnp.where(kd_mask, jnp.concatenate([kb] * A_HEADS, axis=0), jnp.zeros((), BF16))
        qkk = _dot_nt(jnp.concatenate([qb, kb], axis=0), kdt)
        qk_w = qkk[:CHUNK] * decay_w
        npow = jnp.where(strict_w, -(bc_w * decay_w * qkk[CHUNK:]), 0.0)
        return dict(c=c, rows=rows, y=y, qs=qs, ks=ks, gsum=gsum, beta=beta, qk_w=qk_w, npow=npow)

    def pass1_back(f, tinv):
        c, rows, y, qs, ks, gsum, beta, qk_w = (f[n] for n in ("c", "rows", "y", "qs", "ks", "gsum", "beta", "qk_w"))
        t_low = jnp.where(eye_w, 0.0, tinv)
        g_last = gsum[CHUNK - 1:CHUNK, :]
        gl_scr[c] = jnp.exp(g_last)
        for h in range(A_HEADS):
            hs = slice(h * A_DK, (h + 1) * A_DK)
            cs = slice(h * CHUNK, (h + 1) * CHUNK)
            v = y[:, 2 * A_WIDTH + h * A_DV:2 * A_WIDTH + (h + 1) * A_DV]
            gc = gsum[:, h:h + 1]
            bc = beta[:, A_HEADS + h:A_HEADS + h + 1]
            gam = jnp.exp(gc)
            rhs = jnp.concatenate([v * bc, ks[h] * (bc * gam)], axis=-1)
            uw = rhs + _dot(t_low[:, cs].astype(BF16), rhs.astype(BF16))
            qk_scr[c, h] = qk_w[:, cs].astype(BF16)
            u_scr[rows, hs] = uw[:, :A_DV]
            wq_scr[c, h] = jnp.concatenate([uw[:, A_DV:], qs[h] * gam], axis=0).astype(BF16)
            kt_scr[rows, hs] = (ks[h] * jnp.exp(g_last[:, h:h + 1] - gc)).astype(BF16)

    heads = range(A_HEADS)
    hsl = [slice(h * A_DK, (h + 1) * A_DK) for h in heads]

    def pass2_first(c):
        sts = [state_scr[h] for h in heads]
        wss = [_dot(wq_scr[c, h], sts[h].astype(BF16)) for h in heads]
        return sts, wss

    def pass2_second(c, sts, wss):
        r0 = c * CHUNK if isinstance(c, int) else pl.multiple_of(c * CHUNK, CHUNK)
        rows = pl.ds(r0, CHUNK)
        zv = z_ref[0, rows, :]
        glv = gl_scr[c]
        dbs = [(u_scr[rows, hsl[h]] - wss[h][:CHUNK]).astype(BF16) for h in heads]
        outs = [wss[h][CHUNK:] + _dot(qk_scr[c, h], dbs[h]) for h in heads]
        for h in heads:
            state_scr[h] = sts[h] * glv[:, h:h + 1] + _dot_tn(kt_scr[rows, hsl[h]], dbs[h])
        for h in heads:
            o = outs[h]
            on = o * lax.rsqrt(jnp.mean(o * o, axis=-1, keepdims=True) + EPS) * anorm
            gate = _silu(zv[:, hsl[h]])
            o_ref[0, rows, hsl[h]] = (on * gate).astype(o_ref.dtype)

    n_levels = CHUNK.bit_length() - 1
    n_slots = n_levels + 2

    def pass1_group(i, prev_first):
        pending = []
        if prev_first is not None:
            for t in range(group):
                pending += [("first", t), ("second", t)]
        held = {}

        def slot(k):
            while pending and (len(pending) > (n_slots - 1 - k) * 2 * group // n_slots):
                kind, t = pending.pop(0)
                if kind == "first":
                    held[t] = pass2_first(prev_first + t)
                else:
                    pass2_second(prev_first + t, *held.pop(t))

        fronts = [pass1_front(i * group + t) for t in range(group)]
        slot(0)
        npows = [f["npow"] for f in fronts]
        tinvs = [jnp.where(eye_w, 1.0, n) for n in npows]
        npows = [mm_wide(n, n) for n in npows]
        slot(1)
        for lvl in range(1, n_levels - 1):
            boths = [mm_wide(jnp.concatenate([n, t], axis=0), n) for n, t in zip(npows, tinvs)]
            npows = [b[:CHUNK] for b in boths]
            tinvs = [t + b[CHUNK:] for t, b in zip(tinvs, boths)]
            slot(1 + lvl)
        tinvs = [t + mm_wide(t, n) for n, t in zip(npows, tinvs)]
        slot(n_levels)
        for f, t in zip(fronts, tinvs):
            pass1_back(f, t)
        slot(n_levels + 1)
        assert not pending and not held

    def merged(i, carry):
        pass1_group(i, (i - 1) * group)
        return carry

    def pass2_only(c, carry):
        pass2_second(c, *pass2_first(c))
        return carry

    n_groups = n_chunks // group
    pass1_group(0, None)
    lax.fori_loop(1, n_groups, merged, 0)
    lax.fori_loop(n_chunks - group, n_chunks, pass2_only, 0)

    @pl.when(j == pl.num_programs(1) - 1)
    def _():
        sout_ref[0] = state_scr[...]


def _deltanet(qkv, z, ab, cw, cb, alog, dtb, anorm, s0, c0, lb):
    b, l, _ = qkv.shape
    n_chunks = lb // CHUNK
    kern = functools.partial(_deltanet_kernel, n_chunks=n_chunks)
    return pl.pallas_call(
        kern,
        grid=(b, l // lb),
        in_specs=[pl.BlockSpec((1, lb, 3 * A_WIDTH), lambda i, j: (i, j, 0)),
                  pl.BlockSpec((1, lb, A_WIDTH), lambda i, j: (i, j, 0)),
                  pl.BlockSpec((1, lb, LANES), lambda i, j: (i, j, 0)),
                  _const_spec(cw.shape), _const_spec(cb.shape), _const_spec(alog.shape),
                  _const_spec(dtb.shape), _const_spec(anorm.shape),
                  pl.BlockSpec((1, A_HEADS, A_DK, A_DV), lambda i, j: (i, 0, 0, 0)),
                  pl.BlockSpec((1, SUBLANES, 3 * A_WIDTH), lambda i, j: (i, 0, 0))],
        out_specs=[pl.BlockSpec((1, lb, A_WIDTH), lambda i, j: (i, j, 0)),
                   pl.BlockSpec((1, A_HEADS, A_DK, A_DV), lambda i, j: (i, 0, 0, 0))],
        out_shape=[jax.ShapeDtypeStruct((b, l, A_WIDTH), F32),
                   jax.ShapeDtypeStruct((b, A_HEADS, A_DK, A_DV), F32)],
        scratch_shapes=[pltpu.VMEM((A_HEADS, A_DK, A_DV), F32),
                        pltpu.VMEM((SUBLANES + lb, 3 * A_WIDTH), F32),
                        pltpu.VMEM((lb, A_WIDTH), F32),
                        pltpu.VMEM((n_chunks, A_HEADS, 2 * CHUNK, A_DK), BF16),
                        pltpu.VMEM((lb, A_WIDTH), BF16),
                        pltpu.VMEM((n_chunks, A_HEADS, CHUNK, CHUNK), BF16),
                        pltpu.VMEM((n_chunks, 1, LANES), F32)],
        compiler_params=_params("parallel", "arbitrary"),
        name="deltanet",
    )(qkv, z, ab, cw, cb, alog, dtb, anorm, s0, c0)


def _lambda(bl_ref, lam_init):
    bl = bl_ref[...]
    s01 = jnp.sum(bl[0:1] * bl[1:2], axis=-1, keepdims=True)
    s23 = jnp.sum(bl[2:3] * bl[3:4], axis=-1, keepdims=True)
    return jnp.exp(s01) - jnp.exp(s23) + lam_init


def _split_maps(q):
    lane = lax.broadcasted_iota(jnp.int32, q.shape, 1)
    qs = q * (B_DQK ** -0.5)
    q1 = jnp.where(lane < B_DQK, qs, 0.0).astype(BF16)
    q2 = jnp.where(lane >= B_DQK, qs, 0.0).astype(BF16)
    return q1, q2


def _attn_finish(a1, l1, a2, l2, lam, bn, lam_init):
    o = a1 / l1 - lam * (a2 / l2)
    on = o * lax.rsqrt(jnp.mean(o * o, axis=-1, keepdims=True) + EPS) * bn
    return on * (1.0 - lam_init)


ONES_ROWS = 16


ATTN_HEADS_PER_STEP = 2


def _attn_prompt_kernel(q_ref, k_ref, v_ref, bl_ref, bn_ref, o_ref, kb_scr, vt_scr, acc_scr, *, lam_init, tq):
    nh = ATTN_HEADS_PER_STEP
    hd0 = pl.program_id(1) * nh
    qi = pl.program_id(2)
    seq = kb_scr.shape[1]
    nblk = seq // tq

    @pl.when(qi == 0)
    def _():
        ones = jnp.ones((ONES_ROWS, tq), BF16)
        for hh in range(nh):
            kb_scr[hh] = k_ref[pl.ds(hd0 + hh, seq, stride=B_HEADS), :].astype(BF16)
            for i in range(nblk):
                vt = v_ref[pl.ds(hd0 + hh + i * tq * B_HEADS, tq, stride=B_HEADS), :].T.astype(BF16)
                vt_scr[hh, i] = jnp.concatenate([vt, ones], axis=0)

    qmaps = []
    for hh in range(nh):
        qmaps.extend(_split_maps(q_ref[0, :, hh * B_DV:(hh + 1) * B_DV]))
    key = lax.broadcasted_iota(jnp.int32, (tq, tq), 0)
    qry = lax.broadcasted_iota(jnp.int32, (tq, tq), 1)
    shift = CHUNK.bit_length() - 1
    visible = jnp.right_shift(key, shift) <= jnp.right_shift(qry, shift)
    chains = range(2 * nh)

    def step(kb, ms, masked):
        k0 = pl.multiple_of(kb * tq, tq)
        kblks = [kb_scr[hh, pl.ds(k0, tq), :] for hh in range(nh)]
        vts = [vt_scr[hh, kb] for hh in range(nh)]
        sts = [_dot_nt(kblks[t // 2], qmaps[t]) for t in chains]
        if masked:
            sts = [jnp.where(visible, s, NEG_INF) for s in sts]
        mns = [jnp.maximum(m, jnp.max(s, axis=0, keepdims=True)) for m, s in zip(ms, sts)]
        pts = [jnp.exp(s - mn).astype(BF16) for s, mn in zip(sts, mns)]
        upd = [_dot(vts[t // 2], pts[t]) for t in chains]
        for t in chains:
            acc_scr[t] = jnp.exp(ms[t] - mns[t]) * acc_scr[t] + upd[t]
        return tuple(mns)

    acc_scr[...] = jnp.zeros_like(acc_scr)
    m0 = jnp.full((1, tq), NEG_INF, F32)
    ms = lax.fori_loop(0, qi, lambda kb, c: step(kb, c, False), (m0,) * (2 * nh))
    step(qi, ms, True)
    lam = _lambda(bl_ref, lam_init)
    for hh in range(nh):
        a1, a2 = acc_scr[2 * hh], acc_scr[2 * hh + 1]
        ot = a1[:B_DV] / a1[B_DV:B_DV + 1] - lam * (a2[:B_DV] / a2[B_DV:B_DV + 1])
        ont = ot * lax.rsqrt(jnp.mean(ot * ot, axis=0, keepdims=True) + EPS)
        o_ref[0, :, hh * B_DV:(hh + 1) * B_DV] = (ont.T * bn_ref[...] * (1.0 - lam_init)).astype(o_ref.dtype)


def _kv_spec(rows, layer):
    return pl.BlockSpec((None, None, rows, B_DV), lambda i, *_: (layer, i, 0, 0))


def _attn_prompt(bq, kbuf, vbuf, layer, bl, bn, lam_init, tq):
    b, l, _ = bq.shape
    nh = ATTN_HEADS_PER_STEP
    kern = functools.partial(_attn_prompt_kernel, lam_init=lam_init, tq=tq)
    return pl.pallas_call(
        kern,
        grid=(b, B_HEADS // nh, l // tq),
        in_specs=[pl.BlockSpec((1, tq, nh * B_DV), lambda i, h, q: (i, q, h)),
                  _kv_spec(l * B_HEADS, layer), _kv_spec(l * B_HEADS, layer),
                  _const_spec(bl.shape), _const_spec(bn.shape)],
        out_specs=pl.BlockSpec((1, tq, nh * B_DV), lambda i, h, q: (i, q, h)),
        out_shape=jax.ShapeDtypeStruct((b, l, B_WIDTH), F32),
        scratch_shapes=[pltpu.VMEM((nh, l, B_DV), BF16),
                        pltpu.VMEM((nh, l // tq, B_DV + ONES_ROWS, tq), BF16),
                        pltpu.VMEM((2 * nh, B_DV + ONES_ROWS, tq), F32)],
        compiler_params=_params("parallel", "parallel", "arbitrary"),
        name="attn_prompt",
    )(bq, kbuf, vbuf, bl, bn)


def _attn_sample_kernel(q_ref, kp_ref, vp_ref, kn_ref, vn_ref, bl_ref, bn_ref, o_ref, *, lam_init):
    lam = _lambda(bl_ref, lam_init)
    for hd in range(B_HEADS):
        q1, q2 = _split_maps(q_ref[0, :, hd * B_DV:(hd + 1) * B_DV])

        def head_rows(ref):
            return ref[pl.ds(hd, ref.shape[0] // B_HEADS, stride=B_HEADS), :].astype(BF16)

        kp, vp, kn, vn = head_rows(kp_ref), head_rows(vp_ref), head_rows(kn_ref), head_rows(vn_ref)

        def one_map(qm):
            sp = _dot_nt(qm, kp)
            sn = _dot_nt(qm, kn)
            m = jnp.maximum(jnp.max(sp, axis=-1, keepdims=True), jnp.max(sn, axis=-1, keepdims=True))
            pp = jnp.exp(sp - m)
            pn = jnp.exp(sn - m)
            l = jnp.sum(pp, axis=-1, keepdims=True) + jnp.sum(pn, axis=-1, keepdims=True)
            a = _dot(pp.astype(BF16), vp) + _dot(pn.astype(BF16), vn)
            return a, l

        a1, l1 = one_map(q1)
        a2, l2 = one_map(q2)
        o_ref[0, :, hd * B_DV:(hd + 1) * B_DV] = _attn_finish(
            a1, l1, a2, l2, lam, bn_ref[...], lam_init).astype(o_ref.dtype)


def _attn_sample(bq, kbuf, vbuf, past_k, past_v, layer, bl, bn, lam_init):
    b, l, _ = bq.shape
    kern = functools.partial(_attn_sample_kernel, lam_init=lam_init)
    return pl.pallas_call(
        kern,
        grid=(b,),
        in_specs=[pl.BlockSpec((1, l, B_WIDTH), lambda i: (i, 0, 0)),
                  _kv_spec(past_k.shape[2], layer), _kv_spec(past_v.shape[2], layer),
                  _kv_spec(l * B_HEADS, layer), _kv_spec(l * B_HEADS, layer),
                  _const_spec(bl.shape), _const_spec(bn.shape)],
        out_specs=pl.BlockSpec((1, l, B_WIDTH), lambda i: (i, 0, 0)),
        out_shape=jax.ShapeDtypeStruct((b, l, B_WIDTH), F32),
        compiler_params=_params("parallel"),
        name="attn_sample",
    )(bq, past_k, past_v, kbuf, vbuf, bl, bn)


def _pool_rows(halo, x, pos, cw_ref, cs_ref):
    xp = jnp.concatenate([halo, x], axis=0)
    outs = []
    for gi, win in enumerate(C_WINDOWS):
        sl = slice(gi * C_GROUP_DIM, (gi + 1) * C_GROUP_DIM)
        acc = xp[:, sl]
        s = 1
        while s < win:
            acc = acc + pltpu.roll(acc, s, 0)
            s *= 2
        cnt = jnp.minimum(pos + 1, win).astype(F32)
        pooled = acc[HALO:, :] / cnt - x[:, sl]
        outs.append(_dot(pooled.astype(BF16), cw_ref[gi]) * cs_ref[:, sl])
    return jnp.concatenate(outs, axis=-1)


def _outffn_kernel(x_ref, oa_ref, ob_ref, cx_ref, pw_ref, ps_ref, p0_ref, wo_ref, gf_ref, up_ref, fw_ref, fb_ref,
                   dn_ref, f0_ref, gfin_ref, y_ref, fnew_ref, halo_scr, pool_scr,
                   *, seg_len, tiles_per_seq, pos0, final_norm):
    i = pl.program_id(0)
    tm = x_ref.shape[0]
    nseg = tm // seg_len
    d_ff = dn_ref.shape[0]
    hw = FFN_CONV - 1
    carried = tiles_per_seq > 1

    if carried:
        @pl.when(i % tiles_per_seq == 0)
        def _():
            halo_scr[...] = f0_ref[0]
            pool_scr[...] = p0_ref[0]

    row = lax.broadcasted_iota(jnp.int32, (seg_len, 1), 0)
    ocs = []
    for s in range(nseg):
        cxs = cx_ref[s * seg_len:(s + 1) * seg_len, :]
        pos = pos0 + (i % tiles_per_seq) * tm + row
        ocs.append(_pool_rows(pool_scr[...] if carried else p0_ref[s], cxs, pos, pw_ref, ps_ref))
    if carried:
        pool_scr[...] = cx_ref[tm - HALO:, :]
    oc = ocs[0] if nseg == 1 else jnp.concatenate(ocs, axis=0)

    x1 = x_ref[...]
    x1 = x1 + _dot(oa_ref[...].astype(BF16), wo_ref[0:A_WIDTH, :])
    x1 = x1 + _dot(ob_ref[...].astype(BF16), wo_ref[A_WIDTH:A_WIDTH + B_WIDTH, :])
    x1 = x1 + _dot(oc.astype(BF16), wo_ref[A_WIDTH + B_WIDTH:, :])
    h = x1 * lax.rsqrt(jnp.mean(x1 * x1, axis=-1, keepdims=True) + EPS) * gf_ref[...]
    u = _dot(h.astype(BF16), up_ref[...])

    acts = []
    for s in range(nseg):
        us = u[s * seg_len:(s + 1) * seg_len, :]
        prev = halo_scr[...] if carried else f0_ref[s]
        c = fb_ref[...]
        for t in range(FFN_CONV):
            sh = hw - t
            if sh == 0:
                tap = us
            else:
                rolled = pltpu.roll(us, sh, 0)
                for r in range(sh):
                    rolled = jnp.where(row == r, prev[hw - sh + r:hw - sh + r + 1, :], rolled)
                tap = rolled
            c = c + tap * fw_ref[t:t + 1, :]
        acts.append((_silu(c[:, :d_ff]) * c[:, d_ff:]).astype(BF16))
        fnew_ref[s] = us[seg_len - hw:, :]
    if carried:
        halo_scr[...] = u[tm - hw:, :]
    act = acts[0] if nseg == 1 else jnp.concatenate(acts, axis=0)
    x2 = x1 + _dot(act, dn_ref[...])
    if final_norm:
        x2 = x2 * lax.rsqrt(jnp.mean(x2 * x2, axis=-1, keepdims=True) + EPS) * gfin_ref[...]
    y_ref[...] = x2.astype(y_ref.dtype)


def _outffn(x2d, oa, ob, cx, pw, ps, p0, wo, gf, up, fw, fb, dn, f0, gfin, seq_len, tm, pos0, final_norm):
    t, d = x2d.shape
    nb = f0.shape[0]
    two_ff = up.shape[1]
    seg_len = min(tm, seq_len)
    tiles_per_seq = max(1, seq_len // tm)
    nseg = tm // seg_len
    hw = FFN_CONV - 1
    kern = functools.partial(_outffn_kernel, seg_len=seg_len, tiles_per_seq=tiles_per_seq, pos0=pos0,
                             final_norm=final_norm)
    tok = lambda n: pl.BlockSpec((tm, n), lambda i: (i, 0))
    seq_map = lambda i: (i // tiles_per_seq, 0, 0)
    return pl.pallas_call(
        kern,
        grid=(t // tm,),
        in_specs=[tok(d), tok(A_WIDTH), tok(B_WIDTH), tok(C_WIDTH),
                  _const_spec(pw.shape), _const_spec(ps.shape), pl.BlockSpec((nseg, HALO, C_WIDTH), seq_map),
                  _const_spec(wo.shape), _const_spec(gf.shape), _const_spec(up.shape),
                  _const_spec(fw.shape), _const_spec(fb.shape), _const_spec(dn.shape),
                  pl.BlockSpec((nseg, hw, two_ff), seq_map),
                  _const_spec(gfin.shape)],
        out_specs=[tok(d), pl.BlockSpec((nseg, hw, two_ff), seq_map)],
        out_shape=[jax.ShapeDtypeStruct((t, d), F32),
                   jax.ShapeDtypeStruct((nb, hw, two_ff), F32)],
        scratch_shapes=[pltpu.VMEM((hw, two_ff), F32), pltpu.VMEM((HALO, C_WIDTH), F32)],
        compiler_params=_params("arbitrary"),
        name="outffn",
    )(x2d, oa, ob, cx, pw, ps, p0, wo, gf, up, fw, fb, dn, f0, gfin)


def _pad_rows_top(a, rows):
    return jnp.pad(a, ((0, 0), (rows - a.shape[1], 0), (0, 0)))


def _layer(x, layer, depth, kv_bufs, pos0, s0, conv0, past, pool0, ffn0, wts, lam_init, final_norm, gfin):
    (g_mix, w_in, cw, cb, alog, dtb, anorm, bl, bn, c_w, c_s, wo, g_ffn, up, fw, fb, dn) = wts
    b, l, d = x.shape
    t = b * l
    x2d = x.reshape(t, d)
    tm = min(256, t)
    qkv, z, ab, bq, kbuf, vbuf, cx = _inproj(x2d, g_mix, w_in, tm, layer, depth, kv_bufs)
    r3 = lambda a: a.reshape(b, l, a.shape[-1])
    qkv, z, ab, bq, cx = map(r3, (qkv, z, ab, bq, cx))
    kv4 = lambda a: a.reshape(depth, b, l * B_HEADS, B_DV)

    o_a, new_state = _deltanet(qkv, z, ab, cw, cb, alog, dtb, anorm, s0,
                               _pad_rows_top(conv0, SUBLANES), min(l, 1024))
    if past is None:
        o_b = _attn_prompt(bq, kv4(kbuf), kv4(vbuf), layer, bl, bn, lam_init, min(l, 512))
    else:
        o_b = _attn_sample(bq, kv4(kbuf), kv4(vbuf), *past, layer, bl, bn, lam_init)
    y, new_ffn = _outffn(x2d, o_a.reshape(t, -1), o_b.reshape(t, -1), cx.reshape(t, -1), c_w, c_s,
                         _pad_rows_top(pool0, HALO), wo, g_ffn, up, fw, fb, dn, ffn0, gfin, l, tm, pos0, final_norm)
    new = (new_state, qkv[:, l - (A_CONV - 1):, :], cx[:, l - C_POOL_BUF:, :], new_ffn)
    return y.reshape(b, l, d), (kbuf, vbuf), new


def _prep_weights(l, norm_mix, w_in, a_conv_w, a_conv_b, a_log, a_dt_bias, a_norm, b_lambda, b_norm,
                  c_w, c_scale, w_out, norm_ffn, ffn_up, ffn_conv_w, ffn_conv_b, ffn_down):
    w = w_in[l]
    n_qkvz = 4 * A_WIDTH
    w_ab = jnp.pad(w[:, n_qkvz:n_qkvz + 2 * A_HEADS], ((0, 0), (0, LANES - 2 * A_HEADS)))
    w_packed = jnp.concatenate([w[:, :n_qkvz], w_ab, w[:, n_qkvz + 2 * A_HEADS:]], axis=1).astype(BF16)
    lane_pad = lambda v: jnp.pad(v.reshape(1, -1), ((0, 0), (0, LANES - v.shape[-1])))
    return (norm_mix[l].reshape(1, -1), w_packed, a_conv_w[l], a_conv_b[l].reshape(1, -1),
            lane_pad(a_log[l]), lane_pad(a_dt_bias[l]), a_norm[l].reshape(1, -1),
            b_lambda[l], b_norm[l].reshape(1, -1), c_w[l].astype(BF16), c_scale[l].reshape(1, -1),
            w_out[l].astype(BF16), norm_ffn[l].reshape(1, -1), ffn_up[l].astype(BF16),
            ffn_conv_w[l], ffn_conv_b[l].reshape(1, -1), ffn_down[l].astype(BF16))


def kernel(x_prompt, x_sample, state_delta, cache_qkv_conv, cache_k, cache_v, cache_pool, cache_ffn_conv,
           norm_mix, w_in, a_conv_w, a_conv_b, a_log, a_dt_bias, a_norm, b_lambda, b_norm, c_w, c_scale,
           w_out, norm_ffn, ffn_up, ffn_conv_w, ffn_conv_b, ffn_down, norm_final):
    depth = w_in.shape[0]
    nbp = x_prompt.shape[0]
    two_ff = ffn_up.shape[-1]
    gfin = norm_final.reshape(1, -1)
    rows_form = lambda c: c.reshape(c.shape[:2] + (c.shape[2] * c.shape[3], c.shape[4]))
    past = (rows_form(cache_k), rows_form(cache_v))
    yp, ys = x_prompt, x_sample
    p_kv, s_kv = None, None
    p_new, s_new = [], []
    for l in range(depth):
        lam_init = 0.8 - 0.6 * math.exp(-0.3 * l)
        wts = _prep_weights(l, norm_mix, w_in, a_conv_w, a_conv_b, a_log, a_dt_bias, a_norm, b_lambda,
                            b_norm, c_w, c_scale, w_out, norm_ffn, ffn_up, ffn_conv_w, ffn_conv_b, ffn_down)
        final = l == depth - 1
        yp, p_kv, sp = _layer(
            yp, l, depth, p_kv, 0,
            jnp.zeros((nbp, A_HEADS, A_DK, A_DV), F32),
            jnp.zeros((nbp, A_CONV - 1, 3 * A_WIDTH), F32),
            None,
            jnp.zeros((nbp, C_POOL_BUF, C_WIDTH), F32),
            jnp.zeros((nbp, FFN_CONV - 1, two_ff), F32),
            wts, lam_init, final, gfin)
        ys, s_kv, ss = _layer(
            ys, l, depth, s_kv, cache_k.shape[2], state_delta[l], cache_qkv_conv[l], past,
            cache_pool[l], cache_ffn_conv[l], wts, lam_init, final, gfin)
        p_new.append(sp)
        s_new.append(ss)
    p_delta, p_conv, p_pool, p_ffn = [jnp.stack(t) for t in zip(*p_new)]
    s_delta, s_conv, s_pool, s_ffn = [jnp.stack(t) for t in zip(*s_new)]
    heads_form = lambda buf, x: buf.reshape(depth, x.shape[0], x.shape[1], B_HEADS, B_DV)
    p_k, p_v = (heads_form(buf, x_prompt) for buf in p_kv)
    s_k, s_v = (heads_form(buf, x_sample) for buf in s_kv)
    return (yp, ys, p_delta, s_delta, p_conv, s_conv, p_k, s_k, p_v, s_v, p_pool, s_pool, p_ffn, s_ffn)
```

```python
import functools
import math

import jax
import jax.numpy as jnp
from jax import lax
from jax.experimental import pallas as pl
from jax.experimental.pallas import tpu as pltpu

F32 = jnp.float32
BF16 = jnp.bfloat16

CHUNK = 64
EPS = 1e-6
NEG_INF = -1e30
A_HEADS = 4
A_DK = 128
A_DV = 128
A_WIDTH = A_HEADS * A_DV
A_CONV = 4
B_HEADS = 4
B_DQK = 64
B_DV = 128
B_WIDTH = B_HEADS * B_DV
C_WINDOWS = (2, 4, 8, 16)
C_GROUP_DIM = 128
C_WIDTH = 512
C_POOL_BUF = 15
FFN_CONV = 3
LANES = 128
SUBLANES = 8
HALO = 16
VMEM_LIMIT = 56 * 1024 * 1024
TOKEN_TILE = 256
DELTANET_BLOCK = 1024
ATTN_TILE = 512

PROJ_WIDTHS = (3 * A_WIDTH, A_WIDTH, LANES, B_WIDTH, B_WIDTH, B_WIDTH, C_WIDTH)


def _dot(a, b):
    return jnp.dot(a, b, preferred_element_type=F32)


def _dot_exact(a, b):
    return jnp.dot(a, b, preferred_element_type=F32, precision=lax.Precision.HIGHEST)


def _dot_nt(a, b):
    return lax.dot_general(a, b, (((1,), (1,)), ((), ())), preferred_element_type=F32)


def _dot_tn(a, b):
    return lax.dot_general(a, b, (((0,), (0,)), ((), ())), preferred_element_type=F32)


def _sigmoid(x):
    return 0.5 * (1.0 + jnp.tanh(0.5 * x))


def _silu(x):
    return x * _sigmoid(x)


def _softplus(x):
    return jnp.maximum(x, 0.0) + jnp.log1p(jnp.exp(-jnp.abs(x)))


def _const_spec(shape):
    nd = len(shape)
    return pl.BlockSpec(shape, lambda *_: (0,) * nd, pipeline_mode=pl.Buffered(1))


def _params(*sem):
    return pltpu.CompilerParams(dimension_semantics=sem, vmem_limit_bytes=VMEM_LIMIT)


KV_SLOTS = (4, 5)


def _deltanet_inputs(raw, prev_rows, cw_ref, cbias, ones_dk):
    xp = jnp.concatenate([prev_rows, raw], axis=0)
    y = cbias
    for i in range(A_CONV):
        back = A_CONV - 1 - i
        tap = xp if back == 0 else pltpu.roll(xp, back, 0)
        y = y + tap[SUBLANES:, :] * cw_ref[i:i + 1, :]
    y = _silu(y)
    yqk = y[:, :2 * A_WIDTH]
    sq = (yqk * yqk).astype(BF16)
    parts = []
    for h in range(2 * A_HEADS):
        hs = slice(h * A_DK, (h + 1) * A_DK)
        inv = lax.rsqrt(_dot(sq[:, hs], ones_dk) + EPS)
        parts.append(y[:, hs] * (inv * (A_DK ** -0.5) if h < A_HEADS else inv))
    parts.append(y[:, 2 * A_WIDTH:])
    return jnp.concatenate(parts, axis=-1)


def _inproj_kernel(x_ref, g_ref, w_ref, cw_ref, cb_ref, c0_ref, *refs, layer, seg_len, tiles_per_seq):
    halo_scr = refs[-1]
    cnew_ref = refs[-2]
    out_refs = refs[-2 - len(PROJ_WIDTHS):-2]
    i = pl.program_id(0)
    tm = x_ref.shape[0]
    nseg = tm // seg_len
    carried = tiles_per_seq > 1
    x = x_ref[...]
    h = x * lax.rsqrt(jnp.mean(x * x, axis=-1, keepdims=True) + EPS) * g_ref[...]
    hb = h.astype(BF16)
    if carried:
        @pl.when(i % tiles_per_seq == 0)
        def _():
            halo_scr[...] = c0_ref[0]
    off = 0
    for slot, (ref, n) in enumerate(zip(out_refs, PROJ_WIDTHS)):
        y = _dot(hb, w_ref[:, off:off + n])
        if slot == 0:
            qkv_raw = y
        elif slot in KV_SLOTS:
            if len(ref.shape) == 3:
                for other in range(ref.shape[0]):
                    if other != layer:
                        ref[other] = jnp.zeros(ref.shape[1:], ref.dtype)
                ref = ref.at[layer]
            for hd in range(B_HEADS):
                ref[pl.ds(hd, tm, stride=B_HEADS), :] = y[:, hd * B_DV:(hd + 1) * B_DV]
        else:
            ref[...] = y.astype(ref.dtype)
        off += n
    ones_dk = jnp.ones((A_DK, A_DK), BF16)
    pieces = []
    for s in range(nseg):
        raw = qkv_raw[s * seg_len:(s + 1) * seg_len, :]
        prev = halo_scr[...] if carried else c0_ref[s]
        pieces.append(_deltanet_inputs(raw, prev, cw_ref, cb_ref[...], ones_dk))
        cnew_ref[s] = raw[seg_len - SUBLANES:, :]
    if carried:
        halo_scr[...] = qkv_raw[tm - SUBLANES:, :]
    out_refs[0][...] = pieces[0] if nseg == 1 else jnp.concatenate(pieces, axis=0)


def _inproj(x2d, g, w, cw, cb, c0, seq_len, tm, layer, depth, kv_bufs):
    t, d = x2d.shape
    n_all = w.shape[1]
    seg_len = min(tm, seq_len)
    tiles_per_seq = max(1, seq_len // tm)
    nseg = tm // seg_len
    seq_map = lambda i: (i // tiles_per_seq, 0, 0)
    in_specs = [pl.BlockSpec((tm, d), lambda i: (i, 0)), _const_spec((1, d)), _const_spec((d, n_all)),
                _const_spec(cw.shape), _const_spec(cb.shape),
                pl.BlockSpec((nseg, SUBLANES, 3 * A_WIDTH), seq_map)]
    out_specs, out_shape = [], []
    for slot, n in enumerate(PROJ_WIDTHS):
        if slot in KV_SLOTS:
            if kv_bufs is None:
                out_specs.append(pl.BlockSpec((depth, tm * B_HEADS, B_DV), lambda i: (0, i, 0)))
            else:
                out_specs.append(pl.BlockSpec((None, tm * B_HEADS, B_DV), lambda i: (layer, i, 0)))
            out_shape.append(jax.ShapeDtypeStruct((depth, t * B_HEADS, B_DV), F32))
        else:
            out_specs.append(pl.BlockSpec((tm, n), lambda i: (i, 0)))
            out_shape.append(jax.ShapeDtypeStruct((t, n), F32))
    out_specs.append(pl.BlockSpec((nseg, SUBLANES, 3 * A_WIDTH), seq_map))
    out_shape.append(jax.ShapeDtypeStruct((c0.shape[0], SUBLANES, 3 * A_WIDTH), F32))
    args, aliases = [x2d, g, w, cw, cb, c0], {}
    if kv_bufs is not None:
        for buf, slot in zip(kv_bufs, KV_SLOTS):
            aliases[len(args)] = slot
            args.append(buf)
            in_specs.append(pl.BlockSpec(memory_space=pl.ANY))
    kern = functools.partial(_inproj_kernel, layer=layer, seg_len=seg_len, tiles_per_seq=tiles_per_seq)
    return pl.pallas_call(
        kern,
        grid=(t // tm,),
        in_specs=in_specs,
        out_specs=out_specs,
        out_shape=out_shape,
        input_output_aliases=aliases,
        scratch_shapes=[pltpu.VMEM((SUBLANES, 3 * A_WIDTH), F32)],
        compiler_params=_params("arbitrary"),
        name="inproj",
    )(*args)


def _split_bf16(x):
    hi = x.astype(BF16)
    lo = (x - hi.astype(F32)).astype(BF16)
    return hi, lo


def _deltanet_kernel(qkv_ref, z_ref, ab_ref, alog_ref, dtb_ref, anorm_ref, s0_ref, o_ref, sout_ref,
                     state_scr, u_scr, wq_scr, kt_scr, qk_scr, gl_scr, *, n_chunks):
    j = pl.program_id(1)
    wide = A_HEADS * CHUNK
    log_chunk = CHUNK.bit_length() - 1

    @pl.when(j == 0)
    def _():
        state_scr[...] = s0_ref[0]

    row_w = lax.broadcasted_iota(jnp.int32, (CHUNK, wide), 0)
    lane_w = lax.broadcasted_iota(jnp.int32, (CHUNK, wide), 1)
    col_w = jnp.bitwise_and(lane_w, CHUNK - 1)
    grp_w = jnp.right_shift(lane_w, log_chunk)
    incl_w = row_w >= col_w
    strict_w = row_w > col_w
    eye_w = row_w == col_w
    row_l = lax.broadcasted_iota(jnp.int32, (CHUNK, LANES), 0)
    bd_mask = (jnp.right_shift(lax.broadcasted_iota(jnp.int32, (wide, wide), 0), log_chunk)
               == jnp.right_shift(lax.broadcasted_iota(jnp.int32, (wide, wide), 1), log_chunk))
    kd_mask = (jnp.right_shift(lax.broadcasted_iota(jnp.int32, (wide, A_WIDTH), 0), log_chunk)
               == jnp.right_shift(lax.broadcasted_iota(jnp.int32, (wide, A_WIDTH), 1), A_DK.bit_length() - 1))
    neg_a = -jnp.exp(alog_ref[...])
    dtb = dtb_ref[...]
    anorm = anorm_ref[...]

    def block_diag(xw):
        return jnp.where(bd_mask, jnp.concatenate([xw] * A_HEADS, axis=0), jnp.zeros((), xw.dtype))

    def widen(x, base):
        out = jnp.broadcast_to(x[:, base:base + 1], (CHUNK, wide))
        for h in range(1, A_HEADS):
            out = jnp.where(grp_w == h, x[:, base + h:base + h + 1], out)
        return out

    def mm_wide(x, yw):
        m = x.shape[0]
        xh, xl = _split_bf16(x)
        yh, yl = _split_bf16(yw)
        top = _dot(jnp.concatenate([xh, xl], axis=0), block_diag(yh))
        return top[:m] + top[m:] + _dot(xh, block_diag(yl))

    group = next(g for g in (4, 2, 1) if n_chunks % g == 0)

    def pass1_front(c):
        r0 = c * CHUNK if isinstance(c, int) else pl.multiple_of(c * CHUNK, CHUNK)
        rows = pl.ds(r0, CHUNK)
        y = qkv_ref[0, rows, :]
        qn = y[:, :A_WIDTH]
        kn = y[:, A_WIDTH:2 * A_WIDTH]
        qs = [qn[:, h * A_DK:(h + 1) * A_DK] for h in range(A_HEADS)]
        ks = [kn[:, h * A_DK:(h + 1) * A_DK] for h in range(A_HEADS)]

        abv = ab_ref[0, rows, :]
        g = neg_a * _softplus(abv + dtb)
        beta = _sigmoid(abv)
        gsum = g
        s = 1
        while s < CHUNK:
            gsum = gsum + jnp.where(row_l >= s, pltpu.roll(gsum, s, 0), 0.0)
            s *= 2
        gc_w = widen(gsum, 0)
        bc_w = widen(beta, A_HEADS)
        gr_w = jnp.sum(jnp.where(eye_w, gc_w, 0.0), axis=0, keepdims=True)
        decay_w = jnp.where(incl_w, jnp.exp(jnp.where(incl_w, gc_w - gr_w, 0.0)), 0.0)

        qb = qn.astype(BF16)
        kb = kn.astype(BF16)
        kdt = jnp.where(kd_mask, jnp.concatenate([kb] * A_HEADS, axis=0), jnp.zeros((), BF16))
        qkk = _dot_nt(jnp.concatenate([qb, kb], axis=0), kdt)
        qk_w = qkk[:CHUNK] * decay_w
        npow = jnp.where(strict_w, -(bc_w * decay_w * qkk[CHUNK:]), 0.0)
        return dict(c=c, rows=rows, y=y, qs=qs, ks=ks, gsum=gsum, beta=beta, qk_w=qk_w, npow=npow)

    def pass1_back(f, tinv):
        c, rows, y, qs, ks, gsum, beta, qk_w = (f[n] for n in ("c", "rows", "y", "qs", "ks", "gsum", "beta", "qk_w"))
        t_low = jnp.where(eye_w, 0.0, tinv)
        g_last = gsum[CHUNK - 1:CHUNK, :]
        gl_scr[c] = jnp.exp(g_last)
        for h in range(A_HEADS):
            hs = slice(h * A_DK, (h + 1) * A_DK)
            cs = slice(h * CHUNK, (h + 1) * CHUNK)
            v = y[:, 2 * A_WIDTH + h * A_DV:2 * A_WIDTH + (h + 1) * A_DV]
            gc = gsum[:, h:h + 1]
            bc = beta[:, A_HEADS + h:A_HEADS + h + 1]
            gam = jnp.exp(gc)
            rhs = jnp.concatenate([v * bc, ks[h] * (bc * gam)], axis=-1)
            uw = rhs + _dot(t_low[:, cs].astype(BF16), rhs.astype(BF16))
            qk_scr[c, h] = qk_w[:, cs].astype(BF16)
            u_scr[rows, hs] = uw[:, :A_DV]
            wq_scr[c, h] = jnp.concatenate([uw[:, A_DV:], qs[h] * gam], axis=0).astype(BF16)
            kt_scr[rows, hs] = (ks[h] * jnp.exp(g_last[:, h:h + 1] - gc)).astype(BF16)

    heads = range(A_HEADS)
    hsl = [slice(h * A_DK, (h + 1) * A_DK) for h in heads]

    def pass2_first(c):
        sts = [state_scr[h] for h in heads]
        wss = [_dot(wq_scr[c, h], sts[h].astype(BF16)) for h in heads]
        return sts, wss

    def pass2_second(c, sts, wss):
        r0 = c * CHUNK if isinstance(c, int) else pl.multiple_of(c * CHUNK, CHUNK)
        rows = pl.ds(r0, CHUNK)
        zv = z_ref[0, rows, :]
        glv = gl_scr[c]
        dbs = [(u_scr[rows, hsl[h]] - wss[h][:CHUNK]).astype(BF16) for h in heads]
        outs = [wss[h][CHUNK:] + _dot(qk_scr[c, h], dbs[h]) for h in heads]
        for h in heads:
            state_scr[h] = sts[h] * glv[:, h:h + 1] + _dot_tn(kt_scr[rows, hsl[h]], dbs[h])
        for h in heads:
            o = outs[h]
            on = o * lax.rsqrt(jnp.mean(o * o, axis=-1, keepdims=True) + EPS) * anorm
            gate = _silu(zv[:, hsl[h]])
            o_ref[0, rows, hsl[h]] = (on * gate).astype(o_ref.dtype)

    n_levels = CHUNK.bit_length() - 1
    n_slots = n_levels + 2

    def pass1_group(i, prev_first):
        pending = []
        if prev_first is not None:
            for t in range(group):
                pending += [("first", t), ("second", t)]
        held = {}

        def slot(k):
            while pending and (len(pending) > (n_slots - 1 - k) * 2 * group // n_slots):
                kind, t = pending.pop(0)
                if kind == "first":
                    held[t] = pass2_first(prev_first + t)
                else:
                    pass2_second(prev_first + t, *held.pop(t))

        fronts = [pass1_front(i * group + t) for t in range(group)]
        slot(0)
        npows = [f["npow"] for f in fronts]
        tinvs = [jnp.where(eye_w, 1.0, n) for n in npows]
        npows = [mm_wide(n, n) for n in npows]
        slot(1)
        for lvl in range(1, n_levels - 1):
            boths = [mm_wide(jnp.concatenate([n, t], axis=0), n) for n, t in zip(npows, tinvs)]
            npows = [b[:CHUNK] for b in boths]
            tinvs = [t + b[CHUNK:] for t, b in zip(tinvs, boths)]
            slot(1 + lvl)
        tinvs = [t + mm_wide(t, n) for n, t in zip(npows, tinvs)]
        slot(n_levels)
        for f, t in zip(fronts, tinvs):
            pass1_back(f, t)
        slot(n_levels + 1)
        assert not pending and not held

    def merged(i, carry):
        pass1_group(i, (i - 1) * group)
        return carry

    def pass2_only(c, carry):
        pass2_second(c, *pass2_first(c))
        return carry

    n_groups = n_chunks // group
    pass1_group(0, None)
    lax.fori_loop(1, n_groups, merged, 0)
    lax.fori_loop(n_chunks - group, n_chunks, pass2_only, 0)

    @pl.when(j == pl.num_programs(1) - 1)
    def _():
        sout_ref[0] = state_scr[...]


def _deltanet(qkv, z, ab, alog, dtb, anorm, s0, lb):
    b, l, _ = qkv.shape
    n_chunks = lb // CHUNK
    kern = functools.partial(_deltanet_kernel, n_chunks=n_chunks)
    return pl.pallas_call(
        kern,
        grid=(b, l // lb),
        in_specs=[pl.BlockSpec((1, lb, 3 * A_WIDTH), lambda i, j: (i, j, 0)),
                  pl.BlockSpec((1, lb, A_WIDTH), lambda i, j: (i, j, 0)),
                  pl.BlockSpec((1, lb, LANES), lambda i, j: (i, j, 0)),
                  _const_spec(alog.shape), _const_spec(dtb.shape), _const_spec(anorm.shape),
                  pl.BlockSpec((1, A_HEADS, A_DK, A_DV), lambda i, j: (i, 0, 0, 0))],
        out_specs=[pl.BlockSpec((1, lb, A_WIDTH), lambda i, j: (i, j, 0)),
                   pl.BlockSpec((1, A_HEADS, A_DK, A_DV), lambda i, j: (i, 0, 0, 0))],
        out_shape=[jax.ShapeDtypeStruct((b, l, A_WIDTH), F32),
                   jax.ShapeDtypeStruct((b, A_HEADS, A_DK, A_DV), F32)],
        scratch_shapes=[pltpu.VMEM((A_HEADS, A_DK, A_DV), F32),
                        pltpu.VMEM((lb, A_WIDTH), F32),
                        pltpu.VMEM((n_chunks, A_HEADS, 2 * CHUNK, A_DK), BF16),
                        pltpu.VMEM((lb, A_WIDTH), BF16),
                        pltpu.VMEM((n_chunks, A_HEADS, CHUNK, CHUNK), BF16),
                        pltpu.VMEM((n_chunks, 1, LANES), F32)],
        compiler_params=_params("parallel", "arbitrary"),
        name="deltanet",
    )(qkv, z, ab, alog, dtb, anorm, s0)


def _lambda(bl_ref, lam_init):
    bl = bl_ref[...]
    s01 = jnp.sum(bl[0:1] * bl[1:2], axis=-1, keepdims=True)
    s23 = jnp.sum(bl[2:3] * bl[3:4], axis=-1, keepdims=True)
    return jnp.exp(s01) - jnp.exp(s23) + lam_init


def _split_maps(q):
    lane = lax.broadcasted_iota(jnp.int32, q.shape, 1)
    qs = q * (B_DQK ** -0.5)
    q1 = jnp.where(lane < B_DQK, qs, 0.0).astype(BF16)
    q2 = jnp.where(lane >= B_DQK, qs, 0.0).astype(BF16)
    return q1, q2


def _attn_finish(a1, l1, a2, l2, lam, bn, lam_init):
    o = a1 / l1 - lam * (a2 / l2)
    on = o * lax.rsqrt(jnp.mean(o * o, axis=-1, keepdims=True) + EPS) * bn
    return on * (1.0 - lam_init)


ONES_ROWS = 16


ATTN_HEADS_PER_STEP = 2


def _attn_prompt_kernel(q_ref, k_ref, v_ref, bl_ref, bn_ref, o_ref, kb_scr, vt_scr, acc_scr, *, lam_init, tq):
    nh = ATTN_HEADS_PER_STEP
    hd0 = pl.program_id(1) * nh
    qi = pl.program_id(2)
    seq = kb_scr.shape[1]
    nblk = seq // tq

    @pl.when(qi == 0)
    def _():
        ones = jnp.ones((ONES_ROWS, tq), BF16)
        for hh in range(nh):
            kb_scr[hh] = k_ref[pl.ds(hd0 + hh, seq, stride=B_HEADS), :].astype(BF16)
            for i in range(nblk):
                vt = v_ref[pl.ds(hd0 + hh + i * tq * B_HEADS, tq, stride=B_HEADS), :].T.astype(BF16)
                vt_scr[hh, i] = jnp.concatenate([vt, ones], axis=0)

    qmaps = []
    for hh in range(nh):
        qmaps.extend(_split_maps(q_ref[0, :, hh * B_DV:(hh + 1) * B_DV]))
    key = lax.broadcasted_iota(jnp.int32, (tq, tq), 0)
    qry = lax.broadcasted_iota(jnp.int32, (tq, tq), 1)
    shift = CHUNK.bit_length() - 1
    visible = jnp.right_shift(key, shift) <= jnp.right_shift(qry, shift)
    chains = range(2 * nh)

    def step(kb, ms, masked):
        k0 = pl.multiple_of(kb * tq, tq)
        kblks = [kb_scr[hh, pl.ds(k0, tq), :] for hh in range(nh)]
        vts = [vt_scr[hh, kb] for hh in range(nh)]
        sts = [_dot_nt(kblks[t // 2], qmaps[t]) for t in chains]
        if masked:
            sts = [jnp.where(visible, s, NEG_INF) for s in sts]
        mns = [jnp.maximum(m, jnp.max(s, axis=0, keepdims=True)) for m, s in zip(ms, sts)]
        pts = [jnp.exp(s - mn).astype(BF16) for s, mn in zip(sts, mns)]
        upd = [_dot(vts[t // 2], pts[t]) for t in chains]
        for t in chains:
            acc_scr[t] = jnp.exp(ms[t] - mns[t]) * acc_scr[t] + upd[t]
        return tuple(mns)

    acc_scr[...] = jnp.zeros_like(acc_scr)
    m0 = jnp.full((1, tq), NEG_INF, F32)
    ms = lax.fori_loop(0, qi, lambda kb, c: step(kb, c, False), (m0,) * (2 * nh))
    step(qi, ms, True)
    lam = _lambda(bl_ref, lam_init)
    for hh in range(nh):
        a1, a2 = acc_scr[2 * hh], acc_scr[2 * hh + 1]
        ot = a1[:B_DV] / a1[B_DV:B_DV + 1] - lam * (a2[:B_DV] / a2[B_DV:B_DV + 1])
        ont = ot * lax.rsqrt(jnp.mean(ot * ot, axis=0, keepdims=True) + EPS)
        o_ref[0, :, hh * B_DV:(hh + 1) * B_DV] = (ont.T * bn_ref[...] * (1.0 - lam_init)).astype(o_ref.dtype)


def _kv_spec(rows, layer):
    return pl.BlockSpec((None, None, rows, B_DV), lambda i, *_: (layer, i, 0, 0))


def _attn_prompt(bq, kbuf, vbuf, layer, bl, bn, lam_init, tq):
    b, l, _ = bq.shape
    nh = ATTN_HEADS_PER_STEP
    kern = functools.partial(_attn_prompt_kernel, lam_init=lam_init, tq=tq)
    return pl.pallas_call(
        kern,
        grid=(b, B_HEADS // nh, l // tq),
        in_specs=[pl.BlockSpec((1, tq, nh * B_DV), lambda i, h, q: (i, q, h)),
                  _kv_spec(l * B_HEADS, layer), _kv_spec(l * B_HEADS, layer),
                  _const_spec(bl.shape), _const_spec(bn.shape)],
        out_specs=pl.BlockSpec((1, tq, nh * B_DV), lambda i, h, q: (i, q, h)),
        out_shape=jax.ShapeDtypeStruct((b, l, B_WIDTH), F32),
        scratch_shapes=[pltpu.VMEM((nh, l, B_DV), BF16),
                        pltpu.VMEM((nh, l // tq, B_DV + ONES_ROWS, tq), BF16),
                        pltpu.VMEM((2 * nh, B_DV + ONES_ROWS, tq), F32)],
        compiler_params=_params("parallel", "parallel", "arbitrary"),
        name="attn_prompt",
    )(bq, kbuf, vbuf, bl, bn)


def _attn_sample_kernel(q_ref, kp_ref, vp_ref, kn_ref, vn_ref, bl_ref, bn_ref, o_ref, *, lam_init):
    lam = _lambda(bl_ref, lam_init)
    for hd in range(B_HEADS):
        q1, q2 = _split_maps(q_ref[0, :, hd * B_DV:(hd + 1) * B_DV])

        def head_rows(ref):
            return ref[pl.ds(hd, ref.shape[0] // B_HEADS, stride=B_HEADS), :].astype(BF16)

        kp, vp, kn, vn = head_rows(kp_ref), head_rows(vp_ref), head_rows(kn_ref), head_rows(vn_ref)

        def one_map(qm):
            sp = _dot_nt(qm, kp)
            sn = _dot_nt(qm, kn)
            m = jnp.maximum(jnp.max(sp, axis=-1, keepdims=True), jnp.max(sn, axis=-1, keepdims=True))
            pp = jnp.exp(sp - m)
            pn = jnp.exp(sn - m)
            l = jnp.sum(pp, axis=-1, keepdims=True) + jnp.sum(pn, axis=-1, keepdims=True)
            a = _dot(pp.astype(BF16), vp) + _dot(pn.astype(BF16), vn)
            return a, l

        a1, l1 = one_map(q1)
        a2, l2 = one_map(q2)
        o_ref[0, :, hd * B_DV:(hd + 1) * B_DV] = _attn_finish(
            a1, l1, a2, l2, lam, bn_ref[...], lam_init).astype(o_ref.dtype)


def _attn_sample(bq, kbuf, vbuf, past_k, past_v, layer, bl, bn, lam_init):
    b, l, _ = bq.shape
    kern = functools.partial(_attn_sample_kernel, lam_init=lam_init)
    return pl.pallas_call(
        kern,
        grid=(b,),
        in_specs=[pl.BlockSpec((1, l, B_WIDTH), lambda i: (i, 0, 0)),
                  _kv_spec(past_k.shape[2], layer), _kv_spec(past_v.shape[2], layer),
                  _kv_spec(l * B_HEADS, layer), _kv_spec(l * B_HEADS, layer),
                  _const_spec(bl.shape), _const_spec(bn.shape)],
        out_specs=pl.BlockSpec((1, l, B_WIDTH), lambda i: (i, 0, 0)),
        out_shape=jax.ShapeDtypeStruct((b, l, B_WIDTH), F32),
        compiler_params=_params("parallel"),
        name="attn_sample",
    )(bq, past_k, past_v, kbuf, vbuf, bl, bn)


def _pool_rows(halo, x, pos, cw_ref, cs_ref):
    xp = jnp.concatenate([halo, x], axis=0)
    outs = []
    for gi, win in enumerate(C_WINDOWS):
        sl = slice(gi * C_GROUP_DIM, (gi + 1) * C_GROUP_DIM)
        acc = xp[:, sl]
        s = 1
        while s < win:
            acc = acc + pltpu.roll(acc, s, 0)
            s *= 2
        cnt = jnp.minimum(pos + 1, win).astype(F32)
        pooled = acc[HALO:, :] / cnt - x[:, sl]
        outs.append(_dot(pooled.astype(BF16), cw_ref[gi]) * cs_ref[:, sl])
    return jnp.concatenate(outs, axis=-1)


def _outffn_kernel(x_ref, oa_ref, ob_ref, cx_ref, pw_ref, ps_ref, p0_ref, wo_ref, gf_ref, up_ref, fw_ref, fb_ref,
                   dn_ref, f0_ref, gfin_ref, y_ref, fnew_ref, halo_scr, pool_scr,
                   *, seg_len, tiles_per_seq, pos0, final_norm):
    i = pl.program_id(0)
    tm = x_ref.shape[0]
    nseg = tm // seg_len
    d_ff = dn_ref.shape[0]
    hw = FFN_CONV - 1
    carried = tiles_per_seq > 1

    if carried:
        @pl.when(i % tiles_per_seq == 0)
        def _():
            halo_scr[...] = f0_ref[0]
            pool_scr[...] = p0_ref[0]

    row = lax.broadcasted_iota(jnp.int32, (seg_len, 1), 0)
    ocs = []
    for s in range(nseg):
        cxs = cx_ref[s * seg_len:(s + 1) * seg_len, :]
        pos = pos0 + (i % tiles_per_seq) * tm + row
        ocs.append(_pool_rows(pool_scr[...] if carried else p0_ref[s], cxs, pos, pw_ref, ps_ref))
    if carried:
        pool_scr[...] = cx_ref[tm - HALO:, :]
    oc = ocs[0] if nseg == 1 else jnp.concatenate(ocs, axis=0)

    x1 = x_ref[...]
    x1 = x1 + _dot(oa_ref[...].astype(BF16), wo_ref[0:A_WIDTH, :])
    x1 = x1 + _dot(ob_ref[...].astype(BF16), wo_ref[A_WIDTH:A_WIDTH + B_WIDTH, :])
    x1 = x1 + _dot(oc.astype(BF16), wo_ref[A_WIDTH + B_WIDTH:, :])
    h = x1 * lax.rsqrt(jnp.mean(x1 * x1, axis=-1, keepdims=True) + EPS) * gf_ref[...]
    u = _dot(h.astype(BF16), up_ref[...])

    acts = []
    for s in range(nseg):
        us = u[s * seg_len:(s + 1) * seg_len, :]
        prev = halo_scr[...] if carried else f0_ref[s]
        c = fb_ref[...]
        for t in range(FFN_CONV):
            sh = hw - t
            if sh == 0:
                tap = us
            else:
                rolled = pltpu.roll(us, sh, 0)
                for r in range(sh):
                    rolled = jnp.where(row == r, prev[hw - sh + r:hw - sh + r + 1, :], rolled)
                tap = rolled
            c = c + tap * fw_ref[t:t + 1, :]
        acts.append((_silu(c[:, :d_ff]) * c[:, d_ff:]).astype(BF16))
        fnew_ref[s] = us[seg_len - hw:, :]
    if carried:
        halo_scr[...] = u[tm - hw:, :]
    act = acts[0] if nseg == 1 else jnp.concatenate(acts, axis=0)
    x2 = x1 + _dot(act, dn_ref[...])
    if final_norm:
        x2 = x2 * lax.rsqrt(jnp.mean(x2 * x2, axis=-1, keepdims=True) + EPS) * gfin_ref[...]
    y_ref[...] = x2.astype(y_ref.dtype)


def _outffn(x2d, oa, ob, cx, pw, ps, p0, wo, gf, up, fw, fb, dn, f0, gfin, seq_len, tm, pos0, final_norm):
    t, d = x2d.shape
    nb = f0.shape[0]
    two_ff = up.shape[1]
    seg_len = min(tm, seq_len)
    tiles_per_seq = max(1, seq_len // tm)
    nseg = tm // seg_len
    hw = FFN_CONV - 1
    kern = functools.partial(_outffn_kernel, seg_len=seg_len, tiles_per_seq=tiles_per_seq, pos0=pos0,
                             final_norm=final_norm)
    tok = lambda n: pl.BlockSpec((tm, n), lambda i: (i, 0))
    seq_map = lambda i: (i // tiles_per_seq, 0, 0)
    return pl.pallas_call(
        kern,
        grid=(t // tm,),
        in_specs=[tok(d), tok(A_WIDTH), tok(B_WIDTH), tok(C_WIDTH),
                  _const_spec(pw.shape), _const_spec(ps.shape), pl.BlockSpec((nseg, HALO, C_WIDTH), seq_map),
                  _const_spec(wo.shape), _const_spec(gf.shape), _const_spec(up.shape),
                  _const_spec(fw.shape), _const_spec(fb.shape), _const_spec(dn.shape),
                  pl.BlockSpec((nseg, hw, two_ff), seq_map),
                  _const_spec(gfin.shape)],
        out_specs=[tok(d), pl.BlockSpec((nseg, hw, two_ff), seq_map)],
        out_shape=[jax.ShapeDtypeStruct((t, d), F32),
                   jax.ShapeDtypeStruct((nb, hw, two_ff), F32)],
        scratch_shapes=[pltpu.VMEM((hw, two_ff), F32), pltpu.VMEM((HALO, C_WIDTH), F32)],
        compiler_params=_params("arbitrary"),
        name="outffn",
    )(x2d, oa, ob, cx, pw, ps, p0, wo, gf, up, fw, fb, dn, f0, gfin)


def _pad_rows_top(a, rows):
    return jnp.pad(a, ((0, 0), (rows - a.shape[1], 0), (0, 0)))


def _layer(x, layer, depth, kv_bufs, pos0, s0, conv0, past, pool0, ffn0, wts, lam_init, final_norm, gfin):
    (g_mix, w_in, cw, cb, alog, dtb, anorm, bl, bn, c_w, c_s, wo, g_ffn, up, fw, fb, dn) = wts
    b, l, d = x.shape
    t = b * l
    x2d = x.reshape(t, d)
    tm = min(TOKEN_TILE, t)
    qkv, z, ab, bq, kbuf, vbuf, cx, conv_rows = _inproj(
        x2d, g_mix, w_in, cw, cb, _pad_rows_top(conv0, SUBLANES), l, tm, layer, depth, kv_bufs)
    r3 = lambda a: a.reshape(b, l, a.shape[-1])
    qkv, z, ab, bq, cx = map(r3, (qkv, z, ab, bq, cx))
    kv4 = lambda a: a.reshape(depth, b, l * B_HEADS, B_DV)

    o_a, new_state = _deltanet(qkv, z, ab, alog, dtb, anorm, s0, min(l, DELTANET_BLOCK))
    if past is None:
        o_b = _attn_prompt(bq, kv4(kbuf), kv4(vbuf), layer, bl, bn, lam_init, min(l, ATTN_TILE))
    else:
        o_b = _attn_sample(bq, kv4(kbuf), kv4(vbuf), *past, layer, bl, bn, lam_init)
    y, new_ffn = _outffn(x2d, o_a.reshape(t, -1), o_b.reshape(t, -1), cx.reshape(t, -1), c_w, c_s,
                         _pad_rows_top(pool0, HALO), wo, g_ffn, up, fw, fb, dn, ffn0, gfin, l, tm, pos0, final_norm)
    new = (new_state, conv_rows[:, SUBLANES - (A_CONV - 1):, :], cx[:, l - C_POOL_BUF:, :], new_ffn)
    return y.reshape(b, l, d), (kbuf, vbuf), new


def _prep_weights(l, norm_mix, w_in, a_conv_w, a_conv_b, a_log, a_dt_bias, a_norm, b_lambda, b_norm,
                  c_w, c_scale, w_out, norm_ffn, ffn_up, ffn_conv_w, ffn_conv_b, ffn_down):
    w = w_in[l]
    n_qkvz = 4 * A_WIDTH
    w_ab = jnp.pad(w[:, n_qkvz:n_qkvz + 2 * A_HEADS], ((0, 0), (0, LANES - 2 * A_HEADS)))
    w_packed = jnp.concatenate([w[:, :n_qkvz], w_ab, w[:, n_qkvz + 2 * A_HEADS:]], axis=1).astype(BF16)
    lane_pad = lambda v: jnp.pad(v.reshape(1, -1), ((0, 0), (0, LANES - v.shape[-1])))
    return (norm_mix[l].reshape(1, -1), w_packed, a_conv_w[l], a_conv_b[l].reshape(1, -1),
            lane_pad(a_log[l]), lane_pad(a_dt_bias[l]), a_norm[l].reshape(1, -1),
            b_lambda[l], b_norm[l].reshape(1, -1), c_w[l].astype(BF16), c_scale[l].reshape(1, -1),
            w_out[l].astype(BF16), norm_ffn[l].reshape(1, -1), ffn_up[l].astype(BF16),
            ffn_conv_w[l], ffn_conv_b[l].reshape(1, -1), ffn_down[l].astype(BF16))


def kernel(x_prompt, x_sample, state_delta, cache_qkv_conv, cache_k, cache_v, cache_pool, cache_ffn_conv,
           norm_mix, w_in, a_conv_w, a_conv_b, a_log, a_dt_bias, a_norm, b_lambda, b_norm, c_w, c_scale,
           w_out, norm_ffn, ffn_up, ffn_conv_w, ffn_conv_b, ffn_down, norm_final):
    depth = w_in.shape[0]
    nbp = x_prompt.shape[0]
    two_ff = ffn_up.shape[-1]
    gfin = norm_final.reshape(1, -1)
    rows_form = lambda c: c.reshape(c.shape[:2] + (c.shape[2] * c.shape[3], c.shape[4]))
    past = (rows_form(cache_k), rows_form(cache_v))
    yp, ys = x_prompt, x_sample
    p_kv, s_kv = None, None
    p_new, s_new = [], []
    for l in range(depth):
        lam_init = 0.8 - 0.6 * math.exp(-0.3 * l)
        wts = _prep_weights(l, norm_mix, w_in, a_conv_w, a_conv_b, a_log, a_dt_bias, a_norm, b_lambda,
                            b_norm, c_w, c_scale, w_out, norm_ffn, ffn_up, ffn_conv_w, ffn_conv_b, ffn_down)
        final = l == depth - 1
        yp, p_kv, sp = _layer(
            yp, l, depth, p_kv, 0,
            jnp.zeros((nbp, A_HEADS, A_DK, A_DV), F32),
            jnp.zeros((nbp, A_CONV - 1, 3 * A_WIDTH), F32),
            None,
            jnp.zeros((nbp, C_POOL_BUF, C_WIDTH), F32),
            jnp.zeros((nbp, FFN_CONV - 1, two_ff), F32),
            wts, lam_init, final, gfin)
        ys, s_kv, ss = _layer(
            ys, l, depth, s_kv, cache_k.shape[2], state_delta[l], cache_qkv_conv[l], past,
            cache_pool[l], cache_ffn_conv[l], wts, lam_init, final, gfin)
        p_new.append(sp)
        s_new.append(ss)
    p_delta, p_conv, p_pool, p_ffn = [jnp.stack(t) for t in zip(*p_new)]
    s_delta, s_conv, s_pool, s_ffn = [jnp.stack(t) for t in zip(*s_new)]
    heads_form = lambda buf, x: buf.reshape(depth, x.shape[0], x.shape[1], B_HEADS, B_DV)
    p_k, p_v = (heads_form(buf, x_prompt) for buf in p_kv)
    s_k, s_v = (heads_form(buf, x_sample) for buf in s_kv)
    return (yp, ys, p_delta, s_delta, p_conv, s_conv, p_k, s_k, p_v, s_v, p_pool, s_pool, p_ffn, s_ffn)
```

```python
import functools
import math

import jax
import jax.numpy as jnp
from jax import lax
from jax.experimental import pallas as pl
from jax.experimental.pallas import tpu as pltpu

F32 = jnp.float32
BF16 = jnp.bfloat16

CHUNK = 64
EPS = 1e-6
NEG_INF = -1e30
A_HEADS = 4
A_DK = 128
A_DV = 128
A_WIDTH = A_HEADS * A_DV
A_CONV = 4
B_HEADS = 4
B_DQK = 64
B_DV = 128
B_WIDTH = B_HEADS * B_DV
C_WINDOWS = (2, 4, 8, 16)
C_GROUP_DIM = 128
C_WIDTH = 512
C_POOL_BUF = 15
FFN_CONV = 3
LANES = 128
SUBLANES = 8
HALO = 16
VMEM_LIMIT = 56 * 1024 * 1024
TOKEN_TILE = 256
DELTANET_BLOCK = 1024
ATTN_TILE = 512
FFN_SUBTILES = 2
PROJ_SUBTILES = 2

PROJ_WIDTHS = (3 * A_WIDTH, A_WIDTH, LANES, B_WIDTH, B_WIDTH, B_WIDTH, C_WIDTH)


def _dot(a, b):
    return jnp.dot(a, b, preferred_element_type=F32)


def _dot_exact(a, b):
    return jnp.dot(a, b, preferred_element_type=F32, precision=lax.Precision.HIGHEST)


def _dot_nt(a, b):
    return lax.dot_general(a, b, (((1,), (1,)), ((), ())), preferred_element_type=F32)


def _dot_tn(a, b):
    return lax.dot_general(a, b, (((0,), (0,)), ((), ())), preferred_element_type=F32)


def _sigmoid(x):
    return 0.5 * (1.0 + jnp.tanh(0.5 * x))


def _silu(x):
    return x * _sigmoid(x)


def _softplus(x):
    return jnp.maximum(x, 0.0) + jnp.log1p(jnp.exp(-jnp.abs(x)))


def _const_spec(shape):
    nd = len(shape)
    return pl.BlockSpec(shape, lambda *_: (0,) * nd, pipeline_mode=pl.Buffered(1))


def _params(*sem):
    return pltpu.CompilerParams(dimension_semantics=sem, vmem_limit_bytes=VMEM_LIMIT)


KV_SLOTS = (4, 5)


def _deltanet_inputs(raw, prev_rows, cw_ref, cbias, ones_dk):
    xp = jnp.concatenate([prev_rows, raw], axis=0)
    y = cbias
    for i in range(A_CONV):
        back = A_CONV - 1 - i
        tap = xp if back == 0 else pltpu.roll(xp, back, 0)
        y = y + tap[SUBLANES:, :] * cw_ref[i:i + 1, :]
    y = _silu(y)
    yqk = y[:, :2 * A_WIDTH]
    sq = (yqk * yqk).astype(BF16)
    parts = []
    for h in range(2 * A_HEADS):
        hs = slice(h * A_DK, (h + 1) * A_DK)
        inv = lax.rsqrt(_dot(sq[:, hs], ones_dk) + EPS)
        parts.append(y[:, hs] * (inv * (A_DK ** -0.5) if h < A_HEADS else inv))
    parts.append(y[:, 2 * A_WIDTH:])
    return jnp.concatenate(parts, axis=-1)


def _inproj_kernel(x_ref, g_ref, w_ref, cw_ref, cb_ref, c0_ref, *refs, layer, sub, seq_len, tiles_per_seq):
    halo_scr = refs[-1]
    cnew_ref = refs[-2]
    out_refs = refs[-2 - len(PROJ_WIDTHS):-2]
    i = pl.program_id(0)
    tm = x_ref.shape[0]
    nsub = tm // sub
    seg_len = min(sub, seq_len)
    nseg = sub // seg_len
    carried = seq_len > sub
    if carried:
        conv_first = jnp.where(i % tiles_per_seq == 0, c0_ref[0], halo_scr[...])

    def normed(k):
        x = x_ref[k * sub:(k + 1) * sub, :]
        h = x * lax.rsqrt(jnp.mean(x * x, axis=-1, keepdims=True) + EPS) * g_ref[...]
        return h.astype(BF16)

    def project(k, hb):
        off = 0
        for slot, (ref, n) in enumerate(zip(out_refs, PROJ_WIDTHS)):
            y = _dot(hb, w_ref[:, off:off + n])
            if slot == 0:
                qkv_raw = y
            elif slot in KV_SLOTS:
                if len(ref.shape) == 3:
                    ref = ref.at[layer]
                for hd in range(B_HEADS):
                    ref[pl.ds(k * sub * B_HEADS + hd, sub, stride=B_HEADS), :] = y[:, hd * B_DV:(hd + 1) * B_DV]
            else:
                ref[k * sub:(k + 1) * sub, :] = y.astype(ref.dtype)
            off += n
        return qkv_raw

    def finish_qkv(k, raw, raw_before):
        pieces = []
        for s in range(nseg):
            rs = raw[s * seg_len:(s + 1) * seg_len, :]
            if carried:
                prev = conv_first if k == 0 else raw_before[sub - SUBLANES:, :]
            else:
                prev = c0_ref[k * nseg + s]
                cnew_ref[k * nseg + s] = rs[seg_len - SUBLANES:, :]
            pieces.append(_deltanet_inputs(rs, prev, cw_ref, cb_ref[...], ones_dk))
        out_refs[0][k * sub:(k + 1) * sub, :] = pieces[0] if nseg == 1 else jnp.concatenate(pieces, axis=0)

    for slot in KV_SLOTS:
        ref = out_refs[slot]
        if len(ref.shape) == 3:
            for other in range(ref.shape[0]):
                if other != layer:
                    ref[other] = jnp.zeros(ref.shape[1:], ref.dtype)
    ones_dk = jnp.ones((A_DK, A_DK), BF16)
    hbs = [normed(k) for k in range(nsub)]
    raws = [project(k, hbs[k]) for k in range(nsub)]
    for k in range(nsub):
        finish_qkv(k, raws[k], raws[k - 1] if k else None)
    if carried:
        halo_scr[...] = raws[-1][sub - SUBLANES:, :]
        cnew_ref[0] = raws[-1][sub - SUBLANES:, :]


def _inproj(x2d, g, w, cw, cb, c0, seq_len, layer, depth, kv_bufs):
    t, d = x2d.shape
    n_all = w.shape[1]
    sub = min(TOKEN_TILE, t)
    tm = sub * PROJ_SUBTILES if t % (sub * PROJ_SUBTILES) == 0 else sub
    assert seq_len % tm == 0 or tm % seq_len == 0
    tiles_per_seq = max(1, seq_len // tm)
    nseg = max(1, tm // seq_len)
    seq_map = lambda i: (i // tiles_per_seq, 0, 0)
    in_specs = [pl.BlockSpec((tm, d), lambda i: (i, 0)), _const_spec((1, d)), _const_spec((d, n_all)),
                _const_spec(cw.shape), _const_spec(cb.shape),
                pl.BlockSpec((nseg, SUBLANES, 3 * A_WIDTH), seq_map)]
    out_specs, out_shape = [], []
    for slot, n in enumerate(PROJ_WIDTHS):
        if slot in KV_SLOTS:
            if kv_bufs is None:
                out_specs.append(pl.BlockSpec((depth, tm * B_HEADS, B_DV), lambda i: (0, i, 0)))
            else:
                out_specs.append(pl.BlockSpec((None, tm * B_HEADS, B_DV), lambda i: (layer, i, 0)))
            out_shape.append(jax.ShapeDtypeStruct((depth, t * B_HEADS, B_DV), F32))
        else:
            out_specs.append(pl.BlockSpec((tm, n), lambda i: (i, 0)))
            out_shape.append(jax.ShapeDtypeStruct((t, n), F32))
    out_specs.append(pl.BlockSpec((nseg, SUBLANES, 3 * A_WIDTH), seq_map))
    out_shape.append(jax.ShapeDtypeStruct((c0.shape[0], SUBLANES, 3 * A_WIDTH), F32))
    args, aliases = [x2d, g, w, cw, cb, c0], {}
    if kv_bufs is not None:
        for buf, slot in zip(kv_bufs, KV_SLOTS):
            aliases[len(args)] = slot
            args.append(buf)
            in_specs.append(pl.BlockSpec(memory_space=pl.ANY))
    kern = functools.partial(_inproj_kernel, layer=layer, sub=sub, seq_len=seq_len, tiles_per_seq=tiles_per_seq)
    return pl.pallas_call(
        kern,
        grid=(t // tm,),
        in_specs=in_specs,
        out_specs=out_specs,
        out_shape=out_shape,
        input_output_aliases=aliases,
        scratch_shapes=[pltpu.VMEM((SUBLANES, 3 * A_WIDTH), F32)],
        compiler_params=_params("arbitrary"),
        name="inproj",
    )(*args)


def _split_bf16(x):
    hi = x.astype(BF16)
    lo = (x - hi.astype(F32)).astype(BF16)
    return hi, lo


def _deltanet_kernel(qkv_ref, z_ref, ab_ref, alog_ref, dtb_ref, anorm_ref, s0_ref, o_ref, sout_ref,
                     state_scr, u_scr, wq_scr, kt_scr, qk_scr, gl_scr, *, n_chunks):
    j = pl.program_id(1)
    wide = A_HEADS * CHUNK
    log_chunk = CHUNK.bit_length() - 1

    @pl.when(j == 0)
    def _():
        state_scr[...] = s0_ref[0]

    row_w = lax.broadcasted_iota(jnp.int32, (CHUNK, wide), 0)
    lane_w = lax.broadcasted_iota(jnp.int32, (CHUNK, wide), 1)
    col_w = jnp.bitwise_and(lane_w, CHUNK - 1)
    grp_w = jnp.right_shift(lane_w, log_chunk)
    incl_w = row_w >= col_w
    strict_w = row_w > col_w
    eye_w = row_w == col_w
    row_l = lax.broadcasted_iota(jnp.int32, (CHUNK, LANES), 0)
    bd_mask = (jnp.right_shift(lax.broadcasted_iota(jnp.int32, (wide, wide), 0), log_chunk)
               == jnp.right_shift(lax.broadcasted_iota(jnp.int32, (wide, wide), 1), log_chunk))
    kd_mask = (jnp.right_shift(lax.broadcasted_iota(jnp.int32, (wide, A_WIDTH), 0), log_chunk)
               == jnp.right_shift(lax.broadcasted_iota(jnp.int32, (wide, A_WIDTH), 1), A_DK.bit_length() - 1))
    neg_a = -jnp.exp(alog_ref[...])
    dtb = dtb_ref[...]
    anorm = anorm_ref[...]

    def block_diag(xw):
        return jnp.where(bd_mask, jnp.concatenate([xw] * A_HEADS, axis=0), jnp.zeros((), xw.dtype))

    def widen(x, base):
        out = jnp.broadcast_to(x[:, base:base + 1], (CHUNK, wide))
        for h in range(1, A_HEADS):
            out = jnp.where(grp_w == h, x[:, base + h:base + h + 1], out)
        return out

    def mm_wide(x, yw):
        m = x.shape[0]
        xh, xl = _split_bf16(x)
        yh, yl = _split_bf16(yw)
        top = _dot(jnp.concatenate([xh, xl], axis=0), block_diag(yh))
        return top[:m] + top[m:] + _dot(xh, block_diag(yl))

    group = next(g for g in (4, 2, 1) if n_chunks % g == 0)

    def pass1_front(c):
        r0 = c * CHUNK if isinstance(c, int) else pl.multiple_of(c * CHUNK, CHUNK)
        rows = pl.ds(r0, CHUNK)
        y = qkv_ref[0, rows, :]
        qn = y[:, :A_WIDTH]
        kn = y[:, A_WIDTH:2 * A_WIDTH]
        qs = [qn[:, h * A_DK:(h + 1) * A_DK] for h in range(A_HEADS)]
        ks = [kn[:, h * A_DK:(h + 1) * A_DK] for h in range(A_HEADS)]

        abv = ab_ref[0, rows, :]
        g = neg_a * _softplus(abv + dtb)
        beta = _sigmoid(abv)
        gsum = g
        s = 1
        while s < CHUNK:
            gsum = gsum + jnp.where(row_l >= s, pltpu.roll(gsum, s, 0), 0.0)
            s *= 2
        gc_w = widen(gsum, 0)
        bc_w = widen(beta, A_HEADS)
        gr_w = jnp.sum(jnp.where(eye_w, gc_w, 0.0), axis=0, keepdims=True)
        decay_w = jnp.where(incl_w, jnp.exp(jnp.where(incl_w, gc_w - gr_w, 0.0)), 0.0)

        qb = qn.astype(BF16)
        kb = kn.astype(BF16)
        kdt = jnp.where(kd_mask, jnp.concatenate([kb] * A_HEADS, axis=0), jnp.zeros((), BF16))
        qkk = _dot_nt(jnp.concatenate([qb, kb], axis=0), kdt)
        qk_w = qkk[:CHUNK] * decay_w
        npow = jnp.where(strict_w, -(bc_w * decay_w * qkk[CHUNK:]), 0.0)
        return dict(c=c, rows=rows, y=y, qs=qs, ks=ks, gsum=gsum, beta=beta, qk_w=qk_w, npow=npow)

    def pass1_back(f, tinv):
        c, rows, y, qs, ks, gsum, beta, qk_w = (f[n] for n in ("c", "rows", "y", "qs", "ks", "gsum", "beta", "qk_w"))
        t_low = jnp.where(eye_w, 0.0, tinv)
        g_last = gsum[CHUNK - 1:CHUNK, :]
        gl_scr[c] = jnp.exp(g_last)
        for h in range(A_HEADS):
            hs = slice(h * A_DK, (h + 1) * A_DK)
            cs = slice(h * CHUNK, (h + 1) * CHUNK)
            v = y[:, 2 * A_WIDTH + h * A_DV:2 * A_WIDTH + (h + 1) * A_DV]
            gc = gsum[:, h:h + 1]
            bc = beta[:, A_HEADS + h:A_HEADS + h + 1]
            gam = jnp.exp(gc)
            rhs = jnp.concatenate([v * bc, ks[h] * (bc * gam)], axis=-1)
            uw = rhs + _dot(t_low[:, cs].astype(BF16), rhs.astype(BF16))
            qk_scr[c, h] = qk_w[:, cs].astype(BF16)
            u_scr[rows, hs] = uw[:, :A_DV]
            wq_scr[c, h] = jnp.concatenate([uw[:, A_DV:], qs[h] * gam], axis=0).astype(BF16)
            kt_scr[rows, hs] = (ks[h] * jnp.exp(g_last[:, h:h + 1] - gc)).astype(BF16)

    heads = range(A_HEADS)
    hsl = [slice(h * A_DK, (h + 1) * A_DK) for h in heads]

    def pass2_first(c):
        sts = [state_scr[h] for h in heads]
        wss = [_dot(wq_scr[c, h], sts[h].astype(BF16)) for h in heads]
        return sts, wss

    def pass2_second(c, sts, wss):
        r0 = c * CHUNK if isinstance(c, int) else pl.multiple_of(c * CHUNK, CHUNK)
        rows = pl.ds(r0, CHUNK)
        zv = z_ref[0, rows, :]
        glv = gl_scr[c]
        dbs = [(u_scr[rows, hsl[h]] - wss[h][:CHUNK]).astype(BF16) for h in heads]
        outs = [wss[h][CHUNK:] + _dot(qk_scr[c, h], dbs[h]) for h in heads]
        for h in heads:
            state_scr[h] = sts[h] * glv[:, h:h + 1] + _dot_tn(kt_scr[rows, hsl[h]], dbs[h])
        for h in heads:
            o = outs[h]
            on = o * lax.rsqrt(jnp.mean(o * o, axis=-1, keepdims=True) + EPS) * anorm
            gate = _silu(zv[:, hsl[h]])
            o_ref[0, rows, hsl[h]] = (on * gate).astype(o_ref.dtype)

    n_levels = CHUNK.bit_length() - 1
    n_slots = n_levels + 2

    def pass1_group(i, prev_first):
        pending = []
        if prev_first is not None:
            for t in range(group):
                pending += [("first", t), ("second", t)]
        held = {}

        def slot(k):
            while pending and (len(pending) > (n_slots - 1 - k) * 2 * group // n_slots):
                kind, t = pending.pop(0)
                if kind == "first":
                    held[t] = pass2_first(prev_first + t)
                else:
                    pass2_second(prev_first + t, *held.pop(t))

        fronts = [pass1_front(i * group + t) for t in range(group)]
        slot(0)
        npows = [f["npow"] for f in fronts]
        tinvs = [jnp.where(eye_w, 1.0, n) for n in npows]
        npows = [mm_wide(n, n) for n in npows]
        slot(1)
        for lvl in range(1, n_levels - 1):
            boths = [mm_wide(jnp.concatenate([n, t], axis=0), n) for n, t in zip(npows, tinvs)]
            npows = [b[:CHUNK] for b in boths]
            tinvs = [t + b[CHUNK:] for t, b in zip(tinvs, boths)]
            slot(1 + lvl)
        tinvs = [t + mm_wide(t, n) for n, t in zip(npows, tinvs)]
        slot(n_levels)
        for f, t in zip(fronts, tinvs):
            pass1_back(f, t)
        slot(n_levels + 1)
        assert not pending and not held

    def merged(i, carry):
        pass1_group(i, (i - 1) * group)
        return carry

    def pass2_only(c, carry):
        pass2_second(c, *pass2_first(c))
        return carry

    n_groups = n_chunks // group
    pass1_group(0, None)
    lax.fori_loop(1, n_groups, merged, 0)
    lax.fori_loop(n_chunks - group, n_chunks, pass2_only, 0)

    @pl.when(j == pl.num_programs(1) - 1)
    def _():
        sout_ref[0] = state_scr[...]


def _deltanet(qkv, z, ab, alog, dtb, anorm, s0, lb):
    b, l, _ = qkv.shape
    n_chunks = lb // CHUNK
    kern = functools.partial(_deltanet_kernel, n_chunks=n_chunks)
    return pl.pallas_call(
        kern,
        grid=(b, l // lb),
        in_specs=[pl.BlockSpec((1, lb, 3 * A_WIDTH), lambda i, j: (i, j, 0)),
                  pl.BlockSpec((1, lb, A_WIDTH), lambda i, j: (i, j, 0)),
                  pl.BlockSpec((1, lb, LANES), lambda i, j: (i, j, 0)),
                  _const_spec(alog.shape), _const_spec(dtb.shape), _const_spec(anorm.shape),
                  pl.BlockSpec((1, A_HEADS, A_DK, A_DV), lambda i, j: (i, 0, 0, 0))],
        out_specs=[pl.BlockSpec((1, lb, A_WIDTH), lambda i, j: (i, j, 0)),
                   pl.BlockSpec((1, A_HEADS, A_DK, A_DV), lambda i, j: (i, 0, 0, 0))],
        out_shape=[jax.ShapeDtypeStruct((b, l, A_WIDTH), F32),
                   jax.ShapeDtypeStruct((b, A_HEADS, A_DK, A_DV), F32)],
        scratch_shapes=[pltpu.VMEM((A_HEADS, A_DK, A_DV), F32),
                        pltpu.VMEM((lb, A_WIDTH), F32),
                        pltpu.VMEM((n_chunks, A_HEADS, 2 * CHUNK, A_DK), BF16),
                        pltpu.VMEM((lb, A_WIDTH), BF16),
                        pltpu.VMEM((n_chunks, A_HEADS, CHUNK, CHUNK), BF16),
                        pltpu.VMEM((n_chunks, 1, LANES), F32)],
        compiler_params=_params("parallel", "arbitrary"),
        name="deltanet",
    )(qkv, z, ab, alog, dtb, anorm, s0)


def _lambda(bl_ref, lam_init):
    bl = bl_ref[...]
    s01 = jnp.sum(bl[0:1] * bl[1:2], axis=-1, keepdims=True)
    s23 = jnp.sum(bl[2:3] * bl[3:4], axis=-1, keepdims=True)
    return jnp.exp(s01) - jnp.exp(s23) + lam_init


def _split_maps(q):
    lane = lax.broadcasted_iota(jnp.int32, q.shape, 1)
    qs = q * (B_DQK ** -0.5)
    q1 = jnp.where(lane < B_DQK, qs, 0.0).astype(BF16)
    q2 = jnp.where(lane >= B_DQK, qs, 0.0).astype(BF16)
    return q1, q2


def _attn_finish(a1, l1, a2, l2, lam, bn, lam_init):
    o = a1 / l1 - lam * (a2 / l2)
    on = o * lax.rsqrt(jnp.mean(o * o, axis=-1, keepdims=True) + EPS) * bn
    return on * (1.0 - lam_init)


ONES_ROWS = 16


ATTN_HEADS_PER_STEP = 2


def _attn_prompt_kernel(q_ref, k_ref, v_ref, bl_ref, bn_ref, o_ref, kb_scr, vt_scr, acc_scr, *, lam_init, tq):
    nh = ATTN_HEADS_PER_STEP
    hd0 = pl.program_id(1) * nh
    qi = pl.program_id(2)
    seq = kb_scr.shape[1]
    nblk = seq // tq

    @pl.when(qi == 0)
    def _():
        ones = jnp.ones((ONES_ROWS, tq), BF16)
        for hh in range(nh):
            kb_scr[hh] = k_ref[pl.ds(hd0 + hh, seq, stride=B_HEADS), :].astype(BF16)
            for i in range(nblk):
                vt = v_ref[pl.ds(hd0 + hh + i * tq * B_HEADS, tq, stride=B_HEADS), :].T.astype(BF16)
                vt_scr[hh, i] = jnp.concatenate([vt, ones], axis=0)

    qmaps = []
    for hh in range(nh):
        qmaps.extend(_split_maps(q_ref[0, :, hh * B_DV:(hh + 1) * B_DV]))
    key = lax.broadcasted_iota(jnp.int32, (tq, tq), 0)
    qry = lax.broadcasted_iota(jnp.int32, (tq, tq), 1)
    shift = CHUNK.bit_length() - 1
    visible = jnp.right_shift(key, shift) <= jnp.right_shift(qry, shift)
    chains = range(2 * nh)

    def step(kb, ms, masked):
        k0 = pl.multiple_of(kb * tq, tq)
        kblks = [kb_scr[hh, pl.ds(k0, tq), :] for hh in range(nh)]
        vts = [vt_scr[hh, kb] for hh in range(nh)]
        sts = [_dot_nt(kblks[t // 2], qmaps[t]) for t in chains]
        if masked:
            sts = [jnp.where(visible, s, NEG_INF) for s in sts]
        mns = [jnp.maximum(m, jnp.max(s, axis=0, keepdims=True)) for m, s in zip(ms, sts)]
        pts = [jnp.exp(s - mn).astype(BF16) for s, mn in zip(sts, mns)]
        upd = [_dot(vts[t // 2], pts[t]) for t in chains]
        for t in chains:
            acc_scr[t] = jnp.exp(ms[t] - mns[t]) * acc_scr[t] + upd[t]
        return tuple(mns)

    acc_scr[...] = jnp.zeros_like(acc_scr)
    m0 = jnp.full((1, tq), NEG_INF, F32)
    ms = lax.fori_loop(0, qi, lambda kb, c: step(kb, c, False), (m0,) * (2 * nh))
    step(qi, ms, True)
    lam = _lambda(bl_ref, lam_init)
    for hh in range(nh):
        a1, a2 = acc_scr[2 * hh], acc_scr[2 * hh + 1]
        ot = a1[:B_DV] / a1[B_DV:B_DV + 1] - lam * (a2[:B_DV] / a2[B_DV:B_DV + 1])
        ont = ot * lax.rsqrt(jnp.mean(ot * ot, axis=0, keepdims=True) + EPS)
        o_ref[0, :, hh * B_DV:(hh + 1) * B_DV] = (ont.T * bn_ref[...] * (1.0 - lam_init)).astype(o_ref.dtype)


def _kv_spec(rows, layer):
    return pl.BlockSpec((None, None, rows, B_DV), lambda i, *_: (layer, i, 0, 0))


def _attn_prompt(bq, kbuf, vbuf, layer, bl, bn, lam_init, tq):
    b, l, _ = bq.shape
    nh = ATTN_HEADS_PER_STEP
    kern = functools.partial(_attn_prompt_kernel, lam_init=lam_init, tq=tq)
    return pl.pallas_call(
        kern,
        grid=(b, B_HEADS // nh, l // tq),
        in_specs=[pl.BlockSpec((1, tq, nh * B_DV), lambda i, h, q: (i, q, h)),
                  _kv_spec(l * B_HEADS, layer), _kv_spec(l * B_HEADS, layer),
                  _const_spec(bl.shape), _const_spec(bn.shape)],
        out_specs=pl.BlockSpec((1, tq, nh * B_DV), lambda i, h, q: (i, q, h)),
        out_shape=jax.ShapeDtypeStruct((b, l, B_WIDTH), F32),
        scratch_shapes=[pltpu.VMEM((nh, l, B_DV), BF16),
                        pltpu.VMEM((nh, l // tq, B_DV + ONES_ROWS, tq), BF16),
                        pltpu.VMEM((2 * nh, B_DV + ONES_ROWS, tq), F32)],
        compiler_params=_params("parallel", "parallel", "arbitrary"),
        name="attn_prompt",
    )(bq, kbuf, vbuf, bl, bn)


def _attn_sample_kernel(q_ref, kp_ref, vp_ref, kn_ref, vn_ref, bl_ref, bn_ref, o_ref, *, lam_init):
    lam = _lambda(bl_ref, lam_init)
    for hd in range(B_HEADS):
        q1, q2 = _split_maps(q_ref[0, :, hd * B_DV:(hd + 1) * B_DV])

        def head_rows(ref):
            return ref[pl.ds(hd, ref.shape[0] // B_HEADS, stride=B_HEADS), :].astype(BF16)

        kp, vp, kn, vn = head_rows(kp_ref), head_rows(vp_ref), head_rows(kn_ref), head_rows(vn_ref)

        def one_map(qm):
            sp = _dot_nt(qm, kp)
            sn = _dot_nt(qm, kn)
            m = jnp.maximum(jnp.max(sp, axis=-1, keepdims=True), jnp.max(sn, axis=-1, keepdims=True))
            pp = jnp.exp(sp - m)
            pn = jnp.exp(sn - m)
            l = jnp.sum(pp, axis=-1, keepdims=True) + jnp.sum(pn, axis=-1, keepdims=True)
            a = _dot(pp.astype(BF16), vp) + _dot(pn.astype(BF16), vn)
            return a, l

        a1, l1 = one_map(q1)
        a2, l2 = one_map(q2)
        o_ref[0, :, hd * B_DV:(hd + 1) * B_DV] = _attn_finish(
            a1, l1, a2, l2, lam, bn_ref[...], lam_init).astype(o_ref.dtype)


def _attn_sample(bq, kbuf, vbuf, past_k, past_v, layer, bl, bn, lam_init):
    b, l, _ = bq.shape
    kern = functools.partial(_attn_sample_kernel, lam_init=lam_init)
    return pl.pallas_call(
        kern,
        grid=(b,),
        in_specs=[pl.BlockSpec((1, l, B_WIDTH), lambda i: (i, 0, 0)),
                  _kv_spec(past_k.shape[2], layer), _kv_spec(past_v.shape[2], layer),
                  _kv_spec(l * B_HEADS, layer), _kv_spec(l * B_HEADS, layer),
                  _const_spec(bl.shape), _const_spec(bn.shape)],
        out_specs=pl.BlockSpec((1, l, B_WIDTH), lambda i: (i, 0, 0)),
        out_shape=jax.ShapeDtypeStruct((b, l, B_WIDTH), F32),
        compiler_params=_params("parallel"),
        name="attn_sample",
    )(bq, past_k, past_v, kbuf, vbuf, bl, bn)


def _pool_rows(halo, x, pos, cw_ref, cs_ref):
    xp = jnp.concatenate([halo, x], axis=0)
    outs = []
    for gi, win in enumerate(C_WINDOWS):
        sl = slice(gi * C_GROUP_DIM, (gi + 1) * C_GROUP_DIM)
        acc = xp[:, sl]
        s = 1
        while s < win:
            acc = acc + pltpu.roll(acc, s, 0)
            s *= 2
        cnt = jnp.minimum(pos + 1, win).astype(F32)
        pooled = acc[HALO:, :] / cnt - x[:, sl]
        outs.append(_dot(pooled.astype(BF16), cw_ref[gi]) * cs_ref[:, sl])
    return jnp.concatenate(outs, axis=-1)


def _outffn_kernel(x_ref, oa_ref, ob_ref, cx_ref, pw_ref, ps_ref, p0_ref, wo_ref, gf_ref, up_ref, fw_ref, fb_ref,
                   dn_ref, f0_ref, gfin_ref, y_ref, fnew_ref, halo_scr, pool_scr,
                   *, sub, seq_len, tiles_per_seq, pos0, final_norm):
    i = pl.program_id(0)
    tm = x_ref.shape[0]
    nsub = tm // sub
    seg_len = min(sub, seq_len)
    nseg = sub // seg_len
    d_ff = dn_ref.shape[0]
    hw = FFN_CONV - 1
    carried = seq_len > sub
    row = lax.broadcasted_iota(jnp.int32, (seg_len, 1), 0)
    if carried:
        seq_start = i % tiles_per_seq == 0
        pool_first = jnp.where(seq_start, p0_ref[0], pool_scr[...])
        conv_first = jnp.where(seq_start, f0_ref[0], halo_scr[...])

    def out_proj(k):
        ocs = []
        for s in range(nseg):
            r0 = k * sub + s * seg_len
            cxs = cx_ref[r0:r0 + seg_len, :]
            if carried:
                halo = pool_first if k == 0 else cx_ref[k * sub - HALO:k * sub, :]
                pos = pos0 + (i % tiles_per_seq) * tm + k * sub + row
            else:
                halo = p0_ref[k * nseg + s]
                pos = pos0 + row
            ocs.append(_pool_rows(halo, cxs, pos, pw_ref, ps_ref))
        oc = ocs[0] if nseg == 1 else jnp.concatenate(ocs, axis=0)
        rows = slice(k * sub, (k + 1) * sub)
        x1 = x_ref[rows, :]
        x1 = x1 + _dot(oa_ref[rows, :].astype(BF16), wo_ref[0:A_WIDTH, :])
        x1 = x1 + _dot(ob_ref[rows, :].astype(BF16), wo_ref[A_WIDTH:A_WIDTH + B_WIDTH, :])
        return x1 + _dot(oc.astype(BF16), wo_ref[A_WIDTH + B_WIDTH:, :])

    def up_proj(x1):
        h = x1 * lax.rsqrt(jnp.mean(x1 * x1, axis=-1, keepdims=True) + EPS) * gf_ref[...]
        return _dot(h.astype(BF16), up_ref[...])

    def activation(k, u, u_before):
        acts = []
        for s in range(nseg):
            us = u[s * seg_len:(s + 1) * seg_len, :]
            if carried:
                prev = conv_first if k == 0 else u_before[sub - hw:, :]
            else:
                prev = f0_ref[k * nseg + s]
                fnew_ref[k * nseg + s] = us[seg_len - hw:, :]
            c = fb_ref[...]
            for t in range(FFN_CONV):
                sh = hw - t
                if sh == 0:
                    tap = us
                else:
                    rolled = pltpu.roll(us, sh, 0)
                    for r in range(sh):
                        rolled = jnp.where(row == r, prev[hw - sh + r:hw - sh + r + 1, :], rolled)
                    tap = rolled
                c = c + tap * fw_ref[t:t + 1, :]
            acts.append((_silu(c[:, :d_ff]) * c[:, d_ff:]).astype(BF16))
        return acts[0] if nseg == 1 else jnp.concatenate(acts, axis=0)

    def down_proj(k, x1, act):
        x2 = x1 + _dot(act, dn_ref[...])
        if final_norm:
            x2 = x2 * lax.rsqrt(jnp.mean(x2 * x2, axis=-1, keepdims=True) + EPS) * gfin_ref[...]
        y_ref[k * sub:(k + 1) * sub, :] = x2.astype(y_ref.dtype)

    x1s = [out_proj(k) for k in range(nsub)]
    us = [up_proj(x1) for x1 in x1s]
    for k in range(nsub):
        down_proj(k, x1s[k], activation(k, us[k], us[k - 1] if k else None))
    if carried:
        pool_scr[...] = cx_ref[tm - HALO:, :]
        halo_scr[...] = us[-1][sub - hw:, :]
        fnew_ref[0] = us[-1][sub - hw:, :]


def _outffn(x2d, oa, ob, cx, pw, ps, p0, wo, gf, up, fw, fb, dn, f0, gfin, seq_len, pos0, final_norm):
    t, d = x2d.shape
    nb = f0.shape[0]
    two_ff = up.shape[1]
    sub = min(TOKEN_TILE, t)
    tm = sub * FFN_SUBTILES if t % (sub * FFN_SUBTILES) == 0 else sub
    assert seq_len % tm == 0 or tm % seq_len == 0
    tiles_per_seq = max(1, seq_len // tm)
    nseq = max(1, tm // seq_len)
    hw = FFN_CONV - 1
    kern = functools.partial(_outffn_kernel, sub=sub, seq_len=seq_len, tiles_per_seq=tiles_per_seq, pos0=pos0,
                             final_norm=final_norm)
    tok = lambda n: pl.BlockSpec((tm, n), lambda i: (i, 0))
    seq_map = lambda i: (i // tiles_per_seq, 0, 0)
    return pl.pallas_call(
        kern,
        grid=(t // tm,),
        in_specs=[tok(d), tok(A_WIDTH), tok(B_WIDTH), tok(C_WIDTH),
                  _const_spec(pw.shape), _const_spec(ps.shape), pl.BlockSpec((nseq, HALO, C_WIDTH), seq_map),
                  _const_spec(wo.shape), _const_spec(gf.shape), _const_spec(up.shape),
                  _const_spec(fw.shape), _const_spec(fb.shape), _const_spec(dn.shape),
                  pl.BlockSpec((nseq, hw, two_ff), seq_map),
                  _const_spec(gfin.shape)],
        out_specs=[tok(d), pl.BlockSpec((nseq, hw, two_ff), seq_map)],
        out_shape=[jax.ShapeDtypeStruct((t, d), F32),
                   jax.ShapeDtypeStruct((nb, hw, two_ff), F32)],
        scratch_shapes=[pltpu.VMEM((hw, two_ff), F32), pltpu.VMEM((HALO, C_WIDTH), F32)],
        compiler_params=_params("arbitrary"),
        name="outffn",
    )(x2d, oa, ob, cx, pw, ps, p0, wo, gf, up, fw, fb, dn, f0, gfin)


def _pad_rows_top(a, rows):
    return jnp.pad(a, ((0, 0), (rows - a.shape[1], 0), (0, 0)))


def _layer(x, layer, depth, kv_bufs, pos0, s0, conv0, past, pool0, ffn0, wts, lam_init, final_norm, gfin):
    (g_mix, w_in, cw, cb, alog, dtb, anorm, bl, bn, c_w, c_s, wo, g_ffn, up, fw, fb, dn) = wts
    b, l, d = x.shape
    t = b * l
    x2d = x.reshape(t, d)
    qkv, z, ab, bq, kbuf, vbuf, cx, conv_rows = _inproj(
        x2d, g_mix, w_in, cw, cb, _pad_rows_top(conv0, SUBLANES), l, layer, depth, kv_bufs)
    r3 = lambda a: a.reshape(b, l, a.shape[-1])
    qkv, z, ab, bq, cx = map(r3, (qkv, z, ab, bq, cx))
    kv4 = lambda a: a.reshape(depth, b, l * B_HEADS, B_DV)

    o_a, new_state = _deltanet(qkv, z, ab, alog, dtb, anorm, s0, min(l, DELTANET_BLOCK))
    if past is None:
        o_b = _attn_prompt(bq, kv4(kbuf), kv4(vbuf), layer, bl, bn, lam_init, min(l, ATTN_TILE))
    else:
        o_b = _attn_sample(bq, kv4(kbuf), kv4(vbuf), *past, layer, bl, bn, lam_init)
    y, new_ffn = _outffn(x2d, o_a.reshape(t, -1), o_b.reshape(t, -1), cx.reshape(t, -1), c_w, c_s,
                         _pad_rows_top(pool0, HALO), wo, g_ffn, up, fw, fb, dn, ffn0, gfin, l, pos0, final_norm)
    new = (new_state, conv_rows[:, SUBLANES - (A_CONV - 1):, :], cx[:, l - C_POOL_BUF:, :], new_ffn)
    return y.reshape(b, l, d), (kbuf, vbuf), new


def _prep_weights(l, norm_mix, w_in, a_conv_w, a_conv_b, a_log, a_dt_bias, a_norm, b_lambda, b_norm,
                  c_w, c_scale, w_out, norm_ffn, ffn_up, ffn_conv_w, ffn_conv_b, ffn_down):
    w = w_in[l]
    n_qkvz = 4 * A_WIDTH
    w_ab = jnp.pad(w[:, n_qkvz:n_qkvz + 2 * A_HEADS], ((0, 0), (0, LANES - 2 * A_HEADS)))
    w_packed = jnp.concatenate([w[:, :n_qkvz], w_ab, w[:, n_qkvz + 2 * A_HEADS:]], axis=1).astype(BF16)
    lane_pad = lambda v: jnp.pad(v.reshape(1, -1), ((0, 0), (0, LANES - v.shape[-1])))
    return (norm_mix[l].reshape(1, -1), w_packed, a_conv_w[l], a_conv_b[l].reshape(1, -1),
            lane_pad(a_log[l]), lane_pad(a_dt_bias[l]), a_norm[l].reshape(1, -1),
            b_lambda[l], b_norm[l].reshape(1, -1), c_w[l].astype(BF16), c_scale[l].reshape(1, -1),
            w_out[l].astype(BF16), norm_ffn[l].reshape(1, -1), ffn_up[l].astype(BF16),
            ffn_conv_w[l], ffn_conv_b[l].reshape(1, -1), ffn_down[l].astype(BF16))


def kernel(x_prompt, x_sample, state_delta, cache_qkv_conv, cache_k, cache_v, cache_pool, cache_ffn_conv,
           norm_mix, w_in, a_conv_w, a_conv_b, a_log, a_dt_bias, a_norm, b_lambda, b_norm, c_w, c_scale,
           w_out, norm_ffn, ffn_up, ffn_conv_w, ffn_conv_b, ffn_down, norm_final):
    depth = w_in.shape[0]
    nbp = x_prompt.shape[0]
    two_ff = ffn_up.shape[-1]
    gfin = norm_final.reshape(1, -1)
    rows_form = lambda c: c.reshape(c.shape[:2] + (c.shape[2] * c.shape[3], c.shape[4]))
    past = (rows_form(cache_k), rows_form(cache_v))
    yp, ys = x_prompt, x_sample
    p_kv, s_kv = None, None
    p_new, s_new = [], []
    for l in range(depth):
        lam_init = 0.8 - 0.6 * math.exp(-0.3 * l)
        wts = _prep_weights(l, norm_mix, w_in, a_conv_w, a_conv_b, a_log, a_dt_bias, a_norm, b_lambda,
                            b_norm, c_w, c_scale, w_out, norm_ffn, ffn_up, ffn_conv_w, ffn_conv_b, ffn_down)
        final = l == depth - 1
        yp, p_kv, sp = _layer(
            yp, l, depth, p_kv, 0,
            jnp.zeros((nbp, A_HEADS, A_DK, A_DV), F32),
            jnp.zeros((nbp, A_CONV - 1, 3 * A_WIDTH), F32),
            None,
            jnp.zeros((nbp, C_POOL_BUF, C_WIDTH), F32),
            jnp.zeros((nbp, FFN_CONV - 1, two_ff), F32),
            wts, lam_init, final, gfin)
        ys, s_kv, ss = _layer(
            ys, l, depth, s_kv, cache_k.shape[2], state_delta[l], cache_qkv_conv[l], past,
            cache_pool[l], cache_ffn_conv[l], wts, lam_init, final, gfin)
        p_new.append(sp)
        s_new.append(ss)
    p_delta, p_conv, p_pool, p_ffn = [jnp.stack(t) for t in zip(*p_new)]
    s_delta, s_conv, s_pool, s_ffn = [jnp.stack(t) for t in zip(*s_new)]
    heads_form = lambda buf, x: buf.reshape(depth, x.shape[0], x.shape[1], B_HEADS, B_DV)
    p_k, p_v = (heads_form(buf, x_prompt) for buf in p_kv)
    s_k, s_v = (heads_form(buf, x_sample) for buf in s_kv)
    return (yp, ys, p_delta, s_delta, p_conv, s_conv, p_k, s_k, p_v, s_v, p_pool, s_pool, p_ffn, s_ffn)
```

```python
import functools
import math

import jax
import jax.numpy as jnp
from jax import lax
from jax.experimental import pallas as pl
from jax.experimental.pallas import tpu as pltpu

F32 = jnp.float32
BF16 = jnp.bfloat16

CHUNK = 64
EPS = 1e-6
NEG_INF = -1e30
A_HEADS = 4
A_DK = 128
A_DV = 128
A_WIDTH = A_HEADS * A_DV
A_CONV = 4
B_HEADS = 4
B_DQK = 64
B_DV = 128
B_WIDTH = B_HEADS * B_DV
C_WINDOWS = (2, 4, 8, 16)
C_GROUP_DIM = 128
C_WIDTH = 512
C_POOL_BUF = 15
FFN_CONV = 3
LANES = 128
SUBLANES = 8
HALO = 16
VMEM_LIMIT = 56 * 1024 * 1024
TOKEN_TILE = 256
DELTANET_BLOCK = 1024
ATTN_TILE = 512
FFN_SUBTILES = 2
PROJ_SUBTILES = 2

PROJ_WIDTHS = (3 * A_WIDTH, A_WIDTH, LANES, B_WIDTH, B_WIDTH, B_WIDTH, C_WIDTH)


def _dot(a, b):
    return jnp.dot(a, b, preferred_element_type=F32)


def _dot_exact(a, b):
    return jnp.dot(a, b, preferred_element_type=F32, precision=lax.Precision.HIGHEST)


def _dot_nt(a, b):
    return lax.dot_general(a, b, (((1,), (1,)), ((), ())), preferred_element_type=F32)


def _dot_tn(a, b):
    return lax.dot_general(a, b, (((0,), (0,)), ((), ())), preferred_element_type=F32)


def _sigmoid(x):
    return 0.5 * (1.0 + jnp.tanh(0.5 * x))


def _silu(x):
    return x * _sigmoid(x)


def _softplus(x):
    return jnp.maximum(x, 0.0) + jnp.log1p(jnp.exp(-jnp.abs(x)))


def _const_spec(shape):
    nd = len(shape)
    return pl.BlockSpec(shape, lambda *_: (0,) * nd, pipeline_mode=pl.Buffered(1))


def _params(*sem):
    return pltpu.CompilerParams(dimension_semantics=sem, vmem_limit_bytes=VMEM_LIMIT)


KV_SLOTS = (4, 5)


def _deltanet_inputs(raw, prev_rows, cw_ref, cbias, ones_dk):
    xp = jnp.concatenate([prev_rows, raw], axis=0)
    y = cbias
    for i in range(A_CONV):
        back = A_CONV - 1 - i
        tap = xp if back == 0 else pltpu.roll(xp, back, 0)
        y = y + tap[SUBLANES:, :] * cw_ref[i:i + 1, :]
    y = _silu(y)
    yqk = y[:, :2 * A_WIDTH]
    sq = (yqk * yqk).astype(BF16)
    parts = []
    for h in range(2 * A_HEADS):
        hs = slice(h * A_DK, (h + 1) * A_DK)
        inv = lax.rsqrt(_dot(sq[:, hs], ones_dk) + EPS)
        parts.append(y[:, hs] * (inv * (A_DK ** -0.5) if h < A_HEADS else inv))
    parts.append(y[:, 2 * A_WIDTH:])
    return jnp.concatenate(parts, axis=-1)


def _inproj_kernel(x_ref, g_ref, w_ref, cw_ref, cb_ref, c0_ref, *refs, layer, sub, seq_len, tiles_per_seq):
    halo_scr = refs[-1]
    cnew_ref = refs[-2]
    out_refs = refs[-2 - len(PROJ_WIDTHS):-2]
    i = pl.program_id(0)
    tm = x_ref.shape[0]
    nsub = tm // sub
    seg_len = min(sub, seq_len)
    nseg = sub // seg_len
    carried = seq_len > sub
    if carried:
        conv_first = jnp.where(i % tiles_per_seq == 0, c0_ref[0], halo_scr[...])

    def normed(k):
        x = x_ref[k * sub:(k + 1) * sub, :]
        h = x * lax.rsqrt(jnp.mean(x * x, axis=-1, keepdims=True) + EPS) * g_ref[...]
        return h.astype(BF16)

    def project(k, hb):
        off = 0
        for slot, (ref, n) in enumerate(zip(out_refs, PROJ_WIDTHS)):
            y = _dot(hb, w_ref[:, off:off + n])
            if slot == 0:
                qkv_raw = y
            elif slot in KV_SLOTS:
                if len(ref.shape) == 3:
                    ref = ref.at[layer]
                for hd in range(B_HEADS):
                    ref[pl.ds(k * sub * B_HEADS + hd, sub, stride=B_HEADS), :] = y[:, hd * B_DV:(hd + 1) * B_DV]
            else:
                ref[k * sub:(k + 1) * sub, :] = y.astype(ref.dtype)
            off += n
        return qkv_raw

    def finish_qkv(k, raw, raw_before):
        pieces = []
        for s in range(nseg):
            rs = raw[s * seg_len:(s + 1) * seg_len, :]
            if carried:
                prev = conv_first if k == 0 else raw_before[sub - SUBLANES:, :]
            else:
                prev = c0_ref[k * nseg + s]
                cnew_ref[k * nseg + s] = rs[seg_len - SUBLANES:, :]
            pieces.append(_deltanet_inputs(rs, prev, cw_ref, cb_ref[...], ones_dk))
        out_refs[0][k * sub:(k + 1) * sub, :] = pieces[0] if nseg == 1 else jnp.concatenate(pieces, axis=0)

    for slot in KV_SLOTS:
        ref = out_refs[slot]
        if len(ref.shape) == 3:
            for other in range(ref.shape[0]):
                if other != layer:
                    ref[other] = jnp.zeros(ref.shape[1:], ref.dtype)
    ones_dk = jnp.ones((A_DK, A_DK), BF16)
    hbs = [normed(k) for k in range(nsub)]
    raws = [project(k, hbs[k]) for k in range(nsub)]
    for k in range(nsub):
        finish_qkv(k, raws[k], raws[k - 1] if k else None)
    if carried:
        halo_scr[...] = raws[-1][sub - SUBLANES:, :]
        cnew_ref[0] = raws[-1][sub - SUBLANES:, :]


def _inproj(x2d, g, w, cw, cb, c0, seq_len, layer, depth, kv_bufs):
    t, d = x2d.shape
    n_all = w.shape[1]
    sub = min(TOKEN_TILE, t)
    tm = sub * PROJ_SUBTILES if t % (sub * PROJ_SUBTILES) == 0 else sub
    assert seq_len % tm == 0 or tm % seq_len == 0
    tiles_per_seq = max(1, seq_len // tm)
    nseg = max(1, tm // seq_len)
    seq_map = lambda i: (i // tiles_per_seq, 0, 0)
    in_specs = [pl.BlockSpec((tm, d), lambda i: (i, 0)), _const_spec((1, d)), _const_spec((d, n_all)),
                _const_spec(cw.shape), _const_spec(cb.shape),
                pl.BlockSpec((nseg, SUBLANES, 3 * A_WIDTH), seq_map)]
    out_specs, out_shape = [], []
    for slot, n in enumerate(PROJ_WIDTHS):
        if slot in KV_SLOTS:
            if kv_bufs is None:
                out_specs.append(pl.BlockSpec((depth, tm * B_HEADS, B_DV), lambda i: (0, i, 0)))
            else:
                out_specs.append(pl.BlockSpec((None, tm * B_HEADS, B_DV), lambda i: (layer, i, 0)))
            out_shape.append(jax.ShapeDtypeStruct((depth, t * B_HEADS, B_DV), F32))
        else:
            out_specs.append(pl.BlockSpec((tm, n), lambda i: (i, 0)))
            out_shape.append(jax.ShapeDtypeStruct((t, n), F32))
    out_specs.append(pl.BlockSpec((nseg, SUBLANES, 3 * A_WIDTH), seq_map))
    out_shape.append(jax.ShapeDtypeStruct((c0.shape[0], SUBLANES, 3 * A_WIDTH), F32))
    args, aliases = [x2d, g, w, cw, cb, c0], {}
    if kv_bufs is not None:
        for buf, slot in zip(kv_bufs, KV_SLOTS):
            aliases[len(args)] = slot
            args.append(buf)
            in_specs.append(pl.BlockSpec(memory_space=pl.ANY))
    kern = functools.partial(_inproj_kernel, layer=layer, sub=sub, seq_len=seq_len, tiles_per_seq=tiles_per_seq)
    return pl.pallas_call(
        kern,
        grid=(t // tm,),
        in_specs=in_specs,
        out_specs=out_specs,
        out_shape=out_shape,
        input_output_aliases=aliases,
        scratch_shapes=[pltpu.VMEM((SUBLANES, 3 * A_WIDTH), F32)],
        compiler_params=_params("arbitrary"),
        name="inproj",
    )(*args)


def _split_bf16(x):
    hi = x.astype(BF16)
    lo = (x - hi.astype(F32)).astype(BF16)
    return hi, lo


def _deltanet_kernel(qkv_ref, z_ref, ab_ref, alog_ref, dtb_ref, anorm_ref, s0_ref, o_ref, sout_ref,
                     state_scr, u_scr, wq_scr, kt_scr, qk_scr, gl_scr, *, n_chunks):
    j = pl.program_id(1)
    wide = A_HEADS * CHUNK
    log_chunk = CHUNK.bit_length() - 1

    @pl.when(j == 0)
    def _():
        state_scr[...] = s0_ref[0]

    row_w = lax.broadcasted_iota(jnp.int32, (CHUNK, wide), 0)
    lane_w = lax.broadcasted_iota(jnp.int32, (CHUNK, wide), 1)
    col_w = jnp.bitwise_and(lane_w, CHUNK - 1)
    grp_w = jnp.right_shift(lane_w, log_chunk)
    incl_w = row_w >= col_w
    strict_w = row_w > col_w
    eye_w = row_w == col_w
    row_l = lax.broadcasted_iota(jnp.int32, (CHUNK, LANES), 0)
    bd_mask = (jnp.right_shift(lax.broadcasted_iota(jnp.int32, (wide, wide), 0), log_chunk)
               == jnp.right_shift(lax.broadcasted_iota(jnp.int32, (wide, wide), 1), log_chunk))
    kd_mask = (jnp.right_shift(lax.broadcasted_iota(jnp.int32, (wide, A_WIDTH), 0), log_chunk)
               == jnp.right_shift(lax.broadcasted_iota(jnp.int32, (wide, A_WIDTH), 1), A_DK.bit_length() - 1))
    neg_a = -jnp.exp(alog_ref[...])
    dtb = dtb_ref[...]
    anorm = anorm_ref[...]

    def block_diag(xw):
        return jnp.where(bd_mask, jnp.concatenate([xw] * A_HEADS, axis=0), jnp.zeros((), xw.dtype))

    def widen(x, base):
        out = jnp.broadcast_to(x[:, base:base + 1], (CHUNK, wide))
        for h in range(1, A_HEADS):
            out = jnp.where(grp_w == h, x[:, base + h:base + h + 1], out)
        return out

    def mm_wide(x, yw):
        m = x.shape[0]
        xh, xl = _split_bf16(x)
        yh, yl = _split_bf16(yw)
        top = _dot(jnp.concatenate([xh, xl], axis=0), block_diag(yh))
        return top[:m] + top[m:] + _dot(xh, block_diag(yl))

    group = next(g for g in (4, 2, 1) if n_chunks % g == 0)

    def pass1_front(c):
        r0 = c * CHUNK if isinstance(c, int) else pl.multiple_of(c * CHUNK, CHUNK)
        rows = pl.ds(r0, CHUNK)
        y = qkv_ref[0, rows, :]
        qn = y[:, :A_WIDTH]
        kn = y[:, A_WIDTH:2 * A_WIDTH]
        qs = [qn[:, h * A_DK:(h + 1) * A_DK] for h in range(A_HEADS)]
        ks = [kn[:, h * A_DK:(h + 1) * A_DK] for h in range(A_HEADS)]

        abv = ab_ref[0, rows, :]
        g = neg_a * _softplus(abv + dtb)
        beta = _sigmoid(abv)
        gsum = g
        s = 1
        while s < CHUNK:
            gsum = gsum + jnp.where(row_l >= s, pltpu.roll(gsum, s, 0), 0.0)
            s *= 2
        gc_w = widen(gsum, 0)
        bc_w = widen(beta, A_HEADS)
        gr_w = jnp.sum(jnp.where(eye_w, gc_w, 0.0), axis=0, keepdims=True)
        decay_w = jnp.where(incl_w, jnp.exp(jnp.where(incl_w, gc_w - gr_w, 0.0)), 0.0)

        qb = qn.astype(BF16)
        kb = kn.astype(BF16)
        kdt = jnp.where(kd_mask, jnp.concatenate([kb] * A_HEADS, axis=0), jnp.zeros((), BF16))
        qkk = _dot_nt(jnp.concatenate([qb, kb], axis=0), kdt)
        qk_w = qkk[:CHUNK] * decay_w
        npow = jnp.where(strict_w, -(bc_w * decay_w * qkk[CHUNK:]), 0.0)
        return dict(c=c, rows=rows, y=y, qs=qs, ks=ks, gsum=gsum, beta=beta, qk_w=qk_w, npow=npow)

    def pass1_back(f, tinv):
        c, rows, y, qs, ks, gsum, beta, qk_w = (f[n] for n in ("c", "rows", "y", "qs", "ks", "gsum", "beta", "qk_w"))
        t_low = jnp.where(eye_w, 0.0, tinv)
        g_last = gsum[CHUNK - 1:CHUNK, :]
        gl_scr[c] = jnp.exp(g_last)
        for h in range(A_HEADS):
            hs = slice(h * A_DK, (h + 1) * A_DK)
            cs = slice(h * CHUNK, (h + 1) * CHUNK)
            v = y[:, 2 * A_WIDTH + h * A_DV:2 * A_WIDTH + (h + 1) * A_DV]
            gc = gsum[:, h:h + 1]
            bc = beta[:, A_HEADS + h:A_HEADS + h + 1]
            gam = jnp.exp(gc)
            rhs = jnp.concatenate([v * bc, ks[h] * (bc * gam)], axis=-1)
            uw = rhs + _dot(t_low[:, cs].astype(BF16), rhs.astype(BF16))
            qk_scr[c, h] = qk_w[:, cs].astype(BF16)
            u_scr[rows, hs] = uw[:, :A_DV]
            wq_scr[c, h] = jnp.concatenate([uw[:, A_DV:], qs[h] * gam], axis=0).astype(BF16)
            kt_scr[rows, hs] = (ks[h] * jnp.exp(g_last[:, h:h + 1] - gc)).astype(BF16)

    heads = range(A_HEADS)
    hsl = [slice(h * A_DK, (h + 1) * A_DK) for h in heads]

    def pass2_first(c):
        sts = [state_scr[h] for h in heads]
        wss = [_dot(wq_scr[c, h], sts[h].astype(BF16)) for h in heads]
        return sts, wss

    def pass2_second(c, sts, wss):
        r0 = c * CHUNK if isinstance(c, int) else pl.multiple_of(c * CHUNK, CHUNK)
        rows = pl.ds(r0, CHUNK)
        zv = z_ref[0, rows, :]
        glv = gl_scr[c]
        dbs = [(u_scr[rows, hsl[h]] - wss[h][:CHUNK]).astype(BF16) for h in heads]
        outs = [wss[h][CHUNK:] + _dot(qk_scr[c, h], dbs[h]) for h in heads]
        for h in heads:
            state_scr[h] = sts[h] * glv[:, h:h + 1] + _dot_tn(kt_scr[rows, hsl[h]], dbs[h])
        for h in heads:
            o = outs[h]
            on = o * lax.rsqrt(jnp.mean(o * o, axis=-1, keepdims=True) + EPS) * anorm
            gate = _silu(zv[:, hsl[h]])
            o_ref[0, rows, hsl[h]] = (on * gate).astype(o_ref.dtype)

    n_levels = CHUNK.bit_length() - 1
    n_slots = n_levels + 2

    def pass1_group(i, prev_first):
        pending = []
        if prev_first is not None:
            for t in range(group):
                pending += [("first", t), ("second", t)]
        held = {}

        def slot(k):
            while pending and (len(pending) > (n_slots - 1 - k) * 2 * group // n_slots):
                kind, t = pending.pop(0)
                if kind == "first":
                    held[t] = pass2_first(prev_first + t)
                else:
                    pass2_second(prev_first + t, *held.pop(t))

        fronts = [pass1_front(i * group + t) for t in range(group)]
        slot(0)
        npows = [f["npow"] for f in fronts]
        tinvs = [jnp.where(eye_w, 1.0, n) for n in npows]
        npows = [mm_wide(n, n) for n in npows]
        slot(1)
        for lvl in range(1, n_levels - 1):
            boths = [mm_wide(jnp.concatenate([n, t], axis=0), n) for n, t in zip(npows, tinvs)]
            npows = [b[:CHUNK] for b in boths]
            tinvs = [t + b[CHUNK:] for t, b in zip(tinvs, boths)]
            slot(1 + lvl)
        tinvs = [t + mm_wide(t, n) for n, t in zip(npows, tinvs)]
        slot(n_levels)
        for f, t in zip(fronts, tinvs):
            pass1_back(f, t)
        slot(n_levels + 1)
        assert not pending and not held

    def merged(i, carry):
        pass1_group(i, (i - 1) * group)
        return carry

    def pass2_only(c, carry):
        pass2_second(c, *pass2_first(c))
        return carry

    n_groups = n_chunks // group
    pass1_group(0, None)
    lax.fori_loop(1, n_groups, merged, 0)
    lax.fori_loop(n_chunks - group, n_chunks, pass2_only, 0)

    @pl.when(j == pl.num_programs(1) - 1)
    def _():
        sout_ref[0] = state_scr[...]


def _deltanet(qkv, z, ab, alog, dtb, anorm, s0, lb):
    b, l, _ = qkv.shape
    n_chunks = lb // CHUNK
    kern = functools.partial(_deltanet_kernel, n_chunks=n_chunks)
    return pl.pallas_call(
        kern,
        grid=(b, l // lb),
        in_specs=[pl.BlockSpec((1, lb, 3 * A_WIDTH), lambda i, j: (i, j, 0)),
                  pl.BlockSpec((1, lb, A_WIDTH), lambda i, j: (i, j, 0)),
                  pl.BlockSpec((1, lb, LANES), lambda i, j: (i, j, 0)),
                  _const_spec(alog.shape), _const_spec(dtb.shape), _const_spec(anorm.shape),
                  pl.BlockSpec((1, A_HEADS, A_DK, A_DV), lambda i, j: (i, 0, 0, 0))],
        out_specs=[pl.BlockSpec((1, lb, A_WIDTH), lambda i, j: (i, j, 0)),
                   pl.BlockSpec((1, A_HEADS, A_DK, A_DV), lambda i, j: (i, 0, 0, 0))],
        out_shape=[jax.ShapeDtypeStruct((b, l, A_WIDTH), F32),
                   jax.ShapeDtypeStruct((b, A_HEADS, A_DK, A_DV), F32)],
        scratch_shapes=[pltpu.VMEM((A_HEADS, A_DK, A_DV), F32),
                        pltpu.VMEM((lb, A_WIDTH), F32),
                        pltpu.VMEM((n_chunks, A_HEADS, 2 * CHUNK, A_DK), BF16),
                        pltpu.VMEM((lb, A_WIDTH), BF16),
                        pltpu.VMEM((n_chunks, A_HEADS, CHUNK, CHUNK), BF16),
                        pltpu.VMEM((n_chunks, 1, LANES), F32)],
        compiler_params=_params("parallel", "arbitrary"),
        name="deltanet",
    )(qkv, z, ab, alog, dtb, anorm, s0)


def _lambda(bl_ref, lam_init):
    bl = bl_ref[...]
    s01 = jnp.sum(bl[0:1] * bl[1:2], axis=-1, keepdims=True)
    s23 = jnp.sum(bl[2:3] * bl[3:4], axis=-1, keepdims=True)
    return jnp.exp(s01) - jnp.exp(s23) + lam_init


def _split_maps(q):
    lane = lax.broadcasted_iota(jnp.int32, q.shape, 1)
    qs = q * (B_DQK ** -0.5)
    q1 = jnp.where(lane < B_DQK, qs, 0.0).astype(BF16)
    q2 = jnp.where(lane >= B_DQK, qs, 0.0).astype(BF16)
    return q1, q2


def _attn_finish(a1, l1, a2, l2, lam, bn, lam_init):
    o = a1 / l1 - lam * (a2 / l2)
    on = o * lax.rsqrt(jnp.mean(o * o, axis=-1, keepdims=True) + EPS) * bn
    return on * (1.0 - lam_init)


ONES_ROWS = 16


ATTN_HEADS_PER_STEP = 2


def _attn_prompt_kernel(q_ref, k_ref, v_ref, bl_ref, bn_ref, o_ref, kb_scr, vt_scr, acc_scr, *, lam_init, tq):
    nh = ATTN_HEADS_PER_STEP
    hd0 = pl.program_id(1) * nh
    qi = pl.program_id(2)
    seq = kb_scr.shape[1]
    nblk = seq // tq

    @pl.when(qi == 0)
    def _():
        ones = jnp.ones((ONES_ROWS, tq), BF16)
        for hh in range(nh):
            kb_scr[hh] = k_ref[pl.ds(hd0 + hh, seq, stride=B_HEADS), :].astype(BF16)
            for i in range(nblk):
                vt = v_ref[pl.ds(hd0 + hh + i * tq * B_HEADS, tq, stride=B_HEADS), :].T.astype(BF16)
                vt_scr[hh, i] = jnp.concatenate([vt, ones], axis=0)

    qmaps = []
    for hh in range(nh):
        qmaps.extend(_split_maps(q_ref[0, :, hh * B_DV:(hh + 1) * B_DV]))
    key = lax.broadcasted_iota(jnp.int32, (tq, tq), 0)
    qry = lax.broadcasted_iota(jnp.int32, (tq, tq), 1)
    shift = CHUNK.bit_length() - 1
    visible = jnp.right_shift(key, shift) <= jnp.right_shift(qry, shift)
    chains = range(2 * nh)

    def step(kb, ms, masked):
        k0 = pl.multiple_of(kb * tq, tq)
        kblks = [kb_scr[hh, pl.ds(k0, tq), :] for hh in range(nh)]
        vts = [vt_scr[hh, kb] for hh in range(nh)]
        sts = [_dot_nt(kblks[t // 2], qmaps[t]) for t in chains]
        if masked:
            sts = [jnp.where(visible, s, NEG_INF) for s in sts]
        mns = [jnp.maximum(m, jnp.max(s, axis=0, keepdims=True)) for m, s in zip(ms, sts)]
        pts = [jnp.exp(s - mn).astype(BF16) for s, mn in zip(sts, mns)]
        upd = [_dot(vts[t // 2], pts[t]) for t in chains]
        for t in chains:
            acc_scr[t] = jnp.exp(ms[t] - mns[t]) * acc_scr[t] + upd[t]
        return tuple(mns)

    acc_scr[...] = jnp.zeros_like(acc_scr)
    m0 = jnp.full((1, tq), NEG_INF, F32)
    ms = lax.fori_loop(0, qi, lambda kb, c: step(kb, c, False), (m0,) * (2 * nh))
    step(qi, ms, True)
    lam = _lambda(bl_ref, lam_init)
    for hh in range(nh):
        a1, a2 = acc_scr[2 * hh], acc_scr[2 * hh + 1]
        ot = a1[:B_DV] / a1[B_DV:B_DV + 1] - lam * (a2[:B_DV] / a2[B_DV:B_DV + 1])
        ont = ot * lax.rsqrt(jnp.mean(ot * ot, axis=0, keepdims=True) + EPS)
        o_ref[0, :, hh * B_DV:(hh + 1) * B_DV] = (ont.T * bn_ref[...] * (1.0 - lam_init)).astype(o_ref.dtype)


def _kv_spec(rows, layer):
    return pl.BlockSpec((None, None, rows, B_DV), lambda i, *_: (layer, i, 0, 0))


def _attn_prompt(bq, kbuf, vbuf, layer, bl, bn, lam_init, tq):
    b, l, _ = bq.shape
    nh = ATTN_HEADS_PER_STEP
    kern = functools.partial(_attn_prompt_kernel, lam_init=lam_init, tq=tq)
    return pl.pallas_call(
        kern,
        grid=(b, B_HEADS // nh, l // tq),
        in_specs=[pl.BlockSpec((1, tq, nh * B_DV), lambda i, h, q: (i, q, h)),
                  _kv_spec(l * B_HEADS, layer), _kv_spec(l * B_HEADS, layer),
                  _const_spec(bl.shape), _const_spec(bn.shape)],
        out_specs=pl.BlockSpec((1, tq, nh * B_DV), lambda i, h, q: (i, q, h)),
        out_shape=jax.ShapeDtypeStruct((b, l, B_WIDTH), F32),
        scratch_shapes=[pltpu.VMEM((nh, l, B_DV), BF16),
                        pltpu.VMEM((nh, l // tq, B_DV + ONES_ROWS, tq), BF16),
                        pltpu.VMEM((2 * nh, B_DV + ONES_ROWS, tq), F32)],
        compiler_params=_params("parallel", "parallel", "arbitrary"),
        name="attn_prompt",
    )(bq, kbuf, vbuf, bl, bn)


def _attn_sample_kernel(q_ref, kp_ref, vp_ref, kn_ref, vn_ref, bl_ref, bn_ref, o_ref, *, lam_init):
    lam = _lambda(bl_ref, lam_init)
    heads = range(B_HEADS)
    chains = range(2 * B_HEADS)

    def head_rows(ref, hd):
        return ref[pl.ds(hd, ref.shape[0] // B_HEADS, stride=B_HEADS), :].astype(BF16)

    qmaps = []
    for hd in heads:
        qmaps.extend(_split_maps(q_ref[0, :, hd * B_DV:(hd + 1) * B_DV]))
    kps = [head_rows(kp_ref, hd) for hd in heads]
    kns = [head_rows(kn_ref, hd) for hd in heads]
    sps = [_dot_nt(qmaps[t], kps[t // 2]) for t in chains]
    sns = [_dot_nt(qmaps[t], kns[t // 2]) for t in chains]
    ms = [jnp.maximum(jnp.max(sp, axis=-1, keepdims=True), jnp.max(sn, axis=-1, keepdims=True))
          for sp, sn in zip(sps, sns)]
    pps = [jnp.exp(sp - m) for sp, m in zip(sps, ms)]
    pns = [jnp.exp(sn - m) for sn, m in zip(sns, ms)]
    ls = [jnp.sum(pp, axis=-1, keepdims=True) + jnp.sum(pn, axis=-1, keepdims=True) for pp, pn in zip(pps, pns)]
    vps = [head_rows(vp_ref, hd) for hd in heads]
    vns = [head_rows(vn_ref, hd) for hd in heads]
    accs = [_dot(pps[t].astype(BF16), vps[t // 2]) + _dot(pns[t].astype(BF16), vns[t // 2]) for t in chains]
    for hd in heads:
        o_ref[0, :, hd * B_DV:(hd + 1) * B_DV] = _attn_finish(
            accs[2 * hd], ls[2 * hd], accs[2 * hd + 1], ls[2 * hd + 1], lam, bn_ref[...], lam_init).astype(o_ref.dtype)


def _attn_sample(bq, kbuf, vbuf, past_k, past_v, layer, bl, bn, lam_init):
    b, l, _ = bq.shape
    kern = functools.partial(_attn_sample_kernel, lam_init=lam_init)
    return pl.pallas_call(
        kern,
        grid=(b,),
        in_specs=[pl.BlockSpec((1, l, B_WIDTH), lambda i: (i, 0, 0)),
                  _kv_spec(past_k.shape[2], layer), _kv_spec(past_v.shape[2], layer),
                  _kv_spec(l * B_HEADS, layer), _kv_spec(l * B_HEADS, layer),
                  _const_spec(bl.shape), _const_spec(bn.shape)],
        out_specs=pl.BlockSpec((1, l, B_WIDTH), lambda i: (i, 0, 0)),
        out_shape=jax.ShapeDtypeStruct((b, l, B_WIDTH), F32),
        compiler_params=_params("parallel"),
        name="attn_sample",
    )(bq, past_k, past_v, kbuf, vbuf, bl, bn)


def _pool_rows(halo, x, pos, cw_ref, cs_ref):
    xp = jnp.concatenate([halo, x], axis=0)
    outs = []
    for gi, win in enumerate(C_WINDOWS):
        sl = slice(gi * C_GROUP_DIM, (gi + 1) * C_GROUP_DIM)
        acc = xp[:, sl]
        s = 1
        while s < win:
            acc = acc + pltpu.roll(acc, s, 0)
            s *= 2
        cnt = jnp.minimum(pos + 1, win).astype(F32)
        pooled = acc[HALO:, :] / cnt - x[:, sl]
        outs.append(_dot(pooled.astype(BF16), cw_ref[gi]) * cs_ref[:, sl])
    return jnp.concatenate(outs, axis=-1)


def _outffn_kernel(x_ref, oa_ref, ob_ref, cx_ref, pw_ref, ps_ref, p0_ref, wo_ref, gf_ref, up_ref, fw_ref, fb_ref,
                   dn_ref, f0_ref, gfin_ref, y_ref, fnew_ref, halo_scr, pool_scr,
                   *, sub, seq_len, tiles_per_seq, pos0, final_norm):
    i = pl.program_id(0)
    tm = x_ref.shape[0]
    nsub = tm // sub
    seg_len = min(sub, seq_len)
    nseg = sub // seg_len
    d_ff = dn_ref.shape[0]
    hw = FFN_CONV - 1
    carried = seq_len > sub
    row = lax.broadcasted_iota(jnp.int32, (seg_len, 1), 0)
    if carried:
        seq_start = i % tiles_per_seq == 0
        pool_first = jnp.where(seq_start, p0_ref[0], pool_scr[...])
        conv_first = jnp.where(seq_start, f0_ref[0], halo_scr[...])

    def out_proj(k):
        ocs = []
        for s in range(nseg):
            r0 = k * sub + s * seg_len
            cxs = cx_ref[r0:r0 + seg_len, :]
            if carried:
                halo = pool_first if k == 0 else cx_ref[k * sub - HALO:k * sub, :]
                pos = pos0 + (i % tiles_per_seq) * tm + k * sub + row
            else:
                halo = p0_ref[k * nseg + s]
                pos = pos0 + row
            ocs.append(_pool_rows(halo, cxs, pos, pw_ref, ps_ref))
        oc = ocs[0] if nseg == 1 else jnp.concatenate(ocs, axis=0)
        rows = slice(k * sub, (k + 1) * sub)
        x1 = x_ref[rows, :]
        x1 = x1 + _dot(oa_ref[rows, :].astype(BF16), wo_ref[0:A_WIDTH, :])
        x1 = x1 + _dot(ob_ref[rows, :].astype(BF16), wo_ref[A_WIDTH:A_WIDTH + B_WIDTH, :])
        return x1 + _dot(oc.astype(BF16), wo_ref[A_WIDTH + B_WIDTH:, :])

    def up_proj(x1):
        h = x1 * lax.rsqrt(jnp.mean(x1 * x1, axis=-1, keepdims=True) + EPS) * gf_ref[...]
        return _dot(h.astype(BF16), up_ref[...])

    def activation(k, u, u_before):
        acts = []
        for s in range(nseg):
            us = u[s * seg_len:(s + 1) * seg_len, :]
            if carried:
                prev = conv_first if k == 0 else u_before[sub - hw:, :]
            else:
                prev = f0_ref[k * nseg + s]
                fnew_ref[k * nseg + s] = us[seg_len - hw:, :]
            c = fb_ref[...]
            for t in range(FFN_CONV):
                sh = hw - t
                if sh == 0:
                    tap = us
                else:
                    rolled = pltpu.roll(us, sh, 0)
                    for r in range(sh):
                        rolled = jnp.where(row == r, prev[hw - sh + r:hw - sh + r + 1, :], rolled)
                    tap = rolled
                c = c + tap * fw_ref[t:t + 1, :]
            acts.append((_silu(c[:, :d_ff]) * c[:, d_ff:]).astype(BF16))
        return acts[0] if nseg == 1 else jnp.concatenate(acts, axis=0)

    def down_proj(k, x1, act):
        x2 = x1 + _dot(act, dn_ref[...])
        if final_norm:
            x2 = x2 * lax.rsqrt(jnp.mean(x2 * x2, axis=-1, keepdims=True) + EPS) * gfin_ref[...]
        y_ref[k * sub:(k + 1) * sub, :] = x2.astype(y_ref.dtype)

    x1s = [out_proj(k) for k in range(nsub)]
    us = [up_proj(x1) for x1 in x1s]
    for k in range(nsub):
        down_proj(k, x1s[k], activation(k, us[k], us[k - 1] if k else None))
    if carried:
        pool_scr[...] = cx_ref[tm - HALO:, :]
        halo_scr[...] = us[-1][sub - hw:, :]
        fnew_ref[0] = us[-1][sub - hw:, :]


def _outffn(x2d, oa, ob, cx, pw, ps, p0, wo, gf, up, fw, fb, dn, f0, gfin, seq_len, pos0, final_norm):
    t, d = x2d.shape
    nb = f0.shape[0]
    two_ff = up.shape[1]
    sub = min(TOKEN_TILE, t)
    tm = sub * FFN_SUBTILES if t % (sub * FFN_SUBTILES) == 0 else sub
    assert seq_len % tm == 0 or tm % seq_len == 0
    tiles_per_seq = max(1, seq_len // tm)
    nseq = max(1, tm // seq_len)
    hw = FFN_CONV - 1
    kern = functools.partial(_outffn_kernel, sub=sub, seq_len=seq_len, tiles_per_seq=tiles_per_seq, pos0=pos0,
                             final_norm=final_norm)
    tok = lambda n: pl.BlockSpec((tm, n), lambda i: (i, 0))
    seq_map = lambda i: (i // tiles_per_seq, 0, 0)
    return pl.pallas_call(
        kern,
        grid=(t // tm,),
        in_specs=[tok(d), tok(A_WIDTH), tok(B_WIDTH), tok(C_WIDTH),
                  _const_spec(pw.shape), _const_spec(ps.shape), pl.BlockSpec((nseq, HALO, C_WIDTH), seq_map),
                  _const_spec(wo.shape), _const_spec(gf.shape), _const_spec(up.shape),
                  _const_spec(fw.shape), _const_spec(fb.shape), _const_spec(dn.shape),
                  pl.BlockSpec((nseq, hw, two_ff), seq_map),
                  _const_spec(gfin.shape)],
        out_specs=[tok(d), pl.BlockSpec((nseq, hw, two_ff), seq_map)],
        out_shape=[jax.ShapeDtypeStruct((t, d), F32),
                   jax.ShapeDtypeStruct((nb, hw, two_ff), F32)],
        scratch_shapes=[pltpu.VMEM((hw, two_ff), F32), pltpu.VMEM((HALO, C_WIDTH), F32)],
        compiler_params=_params("arbitrary"),
        name="outffn",
    )(x2d, oa, ob, cx, pw, ps, p0, wo, gf, up, fw, fb, dn, f0, gfin)


def _pad_rows_top(a, rows):
    return jnp.pad(a, ((0, 0), (rows - a.shape[1], 0), (0, 0)))


def _layer(x, layer, depth, kv_bufs, pos0, s0, conv0, past, pool0, ffn0, wts, lam_init, final_norm, gfin):
    (g_mix, w_in, cw, cb, alog, dtb, anorm, bl, bn, c_w, c_s, wo, g_ffn, up, fw, fb, dn) = wts
    b, l, d = x.shape
    t = b * l
    x2d = x.reshape(t, d)
    qkv, z, ab, bq, kbuf, vbuf, cx, conv_rows = _inproj(
        x2d, g_mix, w_in, cw, cb, _pad_rows_top(conv0, SUBLANES), l, layer, depth, kv_bufs)
    r3 = lambda a: a.reshape(b, l, a.shape[-1])
    qkv, z, ab, bq, cx = map(r3, (qkv, z, ab, bq, cx))
    kv4 = lambda a: a.reshape(depth, b, l * B_HEADS, B_DV)

    o_a, new_state = _deltanet(qkv, z, ab, alog, dtb, anorm, s0, min(l, DELTANET_BLOCK))
    if past is None:
        o_b = _attn_prompt(bq, kv4(kbuf), kv4(vbuf), layer, bl, bn, lam_init, min(l, ATTN_TILE))
    else:
        o_b = _attn_sample(bq, kv4(kbuf), kv4(vbuf), *past, layer, bl, bn, lam_init)
    y, new_ffn = _outffn(x2d, o_a.reshape(t, -1), o_b.reshape(t, -1), cx.reshape(t, -1), c_w, c_s,
                         _pad_rows_top(pool0, HALO), wo, g_ffn, up, fw, fb, dn, ffn0, gfin, l, pos0, final_norm)
    new = (new_state, conv_rows[:, SUBLANES - (A_CONV - 1):, :], cx[:, l - C_POOL_BUF:, :], new_ffn)
    return y.reshape(b, l, d), (kbuf, vbuf), new


def _prep_weights(l, norm_mix, w_in, a_conv_w, a_conv_b, a_log, a_dt_bias, a_norm, b_lambda, b_norm,
                  c_w, c_scale, w_out, norm_ffn, ffn_up, ffn_conv_w, ffn_conv_b, ffn_down):
    w = w_in[l]
    n_qkvz = 4 * A_WIDTH
    w_ab = jnp.pad(w[:, n_qkvz:n_qkvz + 2 * A_HEADS], ((0, 0), (0, LANES - 2 * A_HEADS)))
    w_packed = jnp.concatenate([w[:, :n_qkvz], w_ab, w[:, n_qkvz + 2 * A_HEADS:]], axis=1).astype(BF16)
    lane_pad = lambda v: jnp.pad(v.reshape(1, -1), ((0, 0), (0, LANES - v.shape[-1])))
    return (norm_mix[l].reshape(1, -1), w_packed, a_conv_w[l], a_conv_b[l].reshape(1, -1),
            lane_pad(a_log[l]), lane_pad(a_dt_bias[l]), a_norm[l].reshape(1, -1),
            b_lambda[l], b_norm[l].reshape(1, -1), c_w[l].astype(BF16), c_scale[l].reshape(1, -1),
            w_out[l].astype(BF16), norm_ffn[l].reshape(1, -1), ffn_up[l].astype(BF16),
            ffn_conv_w[l], ffn_conv_b[l].reshape(1, -1), ffn_down[l].astype(BF16))


def kernel(x_prompt, x_sample, state_delta, cache_qkv_conv, cache_k, cache_v, cache_pool, cache_ffn_conv,
           norm_mix, w_in, a_conv_w, a_conv_b, a_log, a_dt_bias, a_norm, b_lambda, b_norm, c_w, c_scale,
           w_out, norm_ffn, ffn_up, ffn_conv_w, ffn_conv_b, ffn_down, norm_final):
    depth = w_in.shape[0]
    nbp = x_prompt.shape[0]
    two_ff = ffn_up.shape[-1]
    gfin = norm_final.reshape(1, -1)
    rows_form = lambda c: c.reshape(c.shape[:2] + (c.shape[2] * c.shape[3], c.shape[4]))
    past = (rows_form(cache_k), rows_form(cache_v))
    yp, ys = x_prompt, x_sample
    p_kv, s_kv = None, None
    p_new, s_new = [], []
    for l in range(depth):
        lam_init = 0.8 - 0.6 * math.exp(-0.3 * l)
        wts = _prep_weights(l, norm_mix, w_in, a_conv_w, a_conv_b, a_log, a_dt_bias, a_norm, b_lambda,
                            b_norm, c_w, c_scale, w_out, norm_ffn, ffn_up, ffn_conv_w, ffn_conv_b, ffn_down)
        final = l == depth - 1
        yp, p_kv, sp = _layer(
            yp, l, depth, p_kv, 0,
            jnp.zeros((nbp, A_HEADS, A_DK, A_DV), F32),
            jnp.zeros((nbp, A_CONV - 1, 3 * A_WIDTH), F32),
            None,
            jnp.zeros((nbp, C_POOL_BUF, C_WIDTH), F32),
            jnp.zeros((nbp, FFN_CONV - 1, two_ff), F32),
            wts, lam_init, final, gfin)
        ys, s_kv, ss = _layer(
            ys, l, depth, s_kv, cache_k.shape[2], state_delta[l], cache_qkv_conv[l], past,
            cache_pool[l], cache_ffn_conv[l], wts, lam_init, final, gfin)
        p_new.append(sp)
        s_new.append(ss)
    p_delta, p_conv, p_pool, p_ffn = [jnp.stack(t) for t in zip(*p_new)]
    s_delta, s_conv, s_pool, s_ffn = [jnp.stack(t) for t in zip(*s_new)]
    heads_form = lambda buf, x: buf.reshape(depth, x.shape[0], x.shape[1], B_HEADS, B_DV)
    p_k, p_v = (heads_form(buf, x_prompt) for buf in p_kv)
    s_k, s_v = (heads_form(buf, x_sample) for buf in s_kv)
    return (yp, ys, p_delta, s_delta, p_conv, s_conv, p_k, s_k, p_v, s_v, p_pool, s_pool, p_ffn, s_ffn)
```
